```python
import jax
import jax.numpy as jnp
from jax import lax
import numpy as np

D_MODEL = 2048
BATCH = 2
SEQ = 4096
DEPTH = 4
DEC_BATCH = 8
DEC_SEQ = 1
PAST_LEN = 16384
PAGE_SIZE = 128

H_A = 4
DK_A = 128
DV_A = 128
H_B = 4
DK_B = 64
DV_B = 128
GK_RANK = 16
GK_NORM = 16.0
H_C = 8
DH_C = 128
SB_BLOCK = 128
SB_BIAS_INIT = -7.0
CHUNK = 64
W_A = H_A * DV_A
W_B = H_B * DV_B
W_C = H_C * DH_C
W_MIX = W_A + W_B + W_C
D_FF = ((8 * D_MODEL + 3 * 256 - 1) // (3 * 256)) * 256
N_MOD = 6
EPS = 1e-6
IN_WIDTHS = (H_A * DK_A, H_A * DK_A, W_A, W_A,
             H_B * DK_B, H_B * DK_B, W_B, W_B, GK_RANK,
             W_C, W_C, W_C,
             D_MODEL, D_MODEL, D_MODEL)
N_IN = sum(IN_WIDTHS)

kernel_name = 'hybrid_hgrn2_gla_stickbreak_adaln_step'


def rms_norm(x, g):
    xf = x.astype(jnp.float32)
    y = xf * lax.rsqrt(jnp.mean(xf * xf, axis=-1, keepdims=True) + EPS)
    return (y * g.astype(jnp.float32)).astype(x.dtype)


def split_heads(t, n_heads):
    b, l, _ = t.shape
    return t.reshape(b, l, n_heads, -1).transpose(0, 2, 1, 3)


def merge_heads(t):
    b, h, l, d = t.shape
    return t.transpose(0, 2, 1, 3).reshape(b, l, h * d)


def gated_linear_recurrence(q, k, v, log_f, s0):
    b, h, l, dk = q.shape
    dv = v.shape[-1]
    c = CHUNK if l % CHUNK == 0 else l
    n = l // c

    def to_chunks(a):
        return a.reshape(b, h, n, c, a.shape[-1]).transpose(2, 0, 1, 3, 4)

    causal = jnp.tril(jnp.ones((c, c), dtype=bool))[None, None, :, :, None]

    def step(s, inp):
        qc, kc, vc, fc = inp
        cum = jnp.cumsum(fc, axis=2)
        last = cum[:, :, -1:, :]
        o_inter = jnp.einsum('bhtk,bhkv->bhtv', qc * jnp.exp(cum), s)
        diff = cum[:, :, :, None, :] - cum[:, :, None, :, :]
        decay = jnp.exp(jnp.where(causal, diff, -jnp.inf))
        att = jnp.einsum('bhtk,bhsk,bhtsk->bhts', qc, kc, decay)
        o = o_inter + jnp.einsum('bhts,bhsv->bhtv', att, vc)
        s_new = (jnp.exp(last[:, :, 0, :])[..., None] * s
                 + jnp.einsum('bhsk,bhsv->bhkv', kc * jnp.exp(last - cum), vc))
        return s_new, o

    s_fin, o = lax.scan(step, s0, (to_chunks(q), to_chunks(k), to_chunks(v), to_chunks(log_f)))
    o = o.transpose(1, 2, 0, 3, 4).reshape(b, h, l, dv)
    return o, s_fin


def stick_breaking(q, k, v, bias, q_start):
    b, lq, h, d = q.shape
    lk = k.shape[1]
    blk = SB_BLOCK if lq % SB_BLOCK == 0 else lq
    nb = lq // blk
    qf = q.astype(jnp.float32).reshape(b, nb, blk, h, d).transpose(1, 0, 2, 3, 4)
    kf = k.astype(jnp.float32)
    vf = v.astype(jnp.float32)
    bf = bias.astype(jnp.float32)[None, :, None, None]
    k_pos = jnp.arange(lk)
    scale = d ** -0.5

    def one_block(args):
        qb, bi = args
        q_pos = q_start + bi * blk + jnp.arange(blk)
        mask = k_pos[None, :] < q_pos[:, None]
        z = jnp.einsum('bqhd,bkhd->bhqk', qb, kf) * scale + bf
        log_keep = jnp.where(mask, jax.nn.log_sigmoid(-z), 0.0)
        log_after = lax.cumsum(log_keep, axis=3, reverse=True) - log_keep
        w = jnp.where(mask, jnp.exp(jax.nn.log_sigmoid(z) + log_after), 0.0)
        return jnp.einsum('bhqk,bkhd->bqhd', w, vf)

    o = lax.map(one_block, (qf, jnp.arange(nb)))
    return o.transpose(1, 0, 2, 3, 4).reshape(b, lq, h, d)


def layer(x, c, l, p, lb, s_a, s_b, k_past, v_past):
    dt = x.dtype
    b, L, _ = x.shape
    mod = jax.nn.silu(c) @ p['w_ada'][l] + p['b_ada'][l]
    sh_m, sc_m, gt_m, sh_f, sc_f, gt_f = [m[:, None, :] for m in jnp.split(mod, N_MOD, axis=-1)]
    h = rms_norm(x, p['g_mix'][l]) * (1 + sc_m) + sh_m
    split_at = np.cumsum(IN_WIDTHS)[:-1].tolist()
    (qa, fa, ia, ga, qb, kb, vb, gb, rb, qc, kc, vc, ma, mb, mc) = jnp.split(
        h @ p['w_in'][l], split_at, axis=-1)

    lb_h = lb[l].reshape(1, H_A, 1, DK_A)
    fa_h = split_heads(fa, H_A).astype(jnp.float32)
    log_f_a = jnp.logaddexp(jnp.log(lb_h), jnp.log1p(-lb_h) + jax.nn.log_sigmoid(fa_h))
    k_a = (1.0 - lb_h) * jax.nn.sigmoid(-fa_h)
    q_a = split_heads(jax.nn.silu(qa), H_A).astype(jnp.float32) * DK_A ** -0.5
    o_a, s_a_new = gated_linear_recurrence(q_a, k_a, split_heads(ia, H_A).astype(jnp.float32),
                                           log_f_a, s_a.astype(jnp.float32))
    o_a = merge_heads(rms_norm(o_a, p['gn_a'][l])).astype(dt) * jax.nn.silu(ga)

    q_b = split_heads(qb, H_B).astype(jnp.float32) * DK_B ** -0.5
    gk = (rb @ p['w_gk'][l] + p['b_gk'][l]).astype(jnp.float32)
    log_f_b = split_heads(jax.nn.log_sigmoid(gk) / GK_NORM, H_B)
    o_b, s_b_new = gated_linear_recurrence(q_b, split_heads(kb, H_B).astype(jnp.float32),
                                           split_heads(vb, H_B).astype(jnp.float32),
                                           log_f_b, s_b.astype(jnp.float32))
    o_b = merge_heads(rms_norm(o_b, p['gn_b'][l])).astype(dt) * jax.nn.silu(gb)

    q_c = qc.reshape(b, L, H_C, DH_C)
    k_c = kc.reshape(b, L, H_C, DH_C)
    v_c = vc.reshape(b, L, H_C, DH_C)
    if k_past is None:
        k_all, v_all = k_c, v_c
    else:
        k_all = jnp.concatenate([k_past.astype(k_c.dtype), k_c], axis=1)
        v_all = jnp.concatenate([v_past.astype(v_c.dtype), v_c], axis=1)
    o_c = stick_breaking(q_c, k_all, v_all, p['b_sb'][l],
                         k_all.shape[1] - L).reshape(b, L, W_C).astype(dt)

    w_br = p['w_br'][l]
    merged = (jax.nn.sigmoid(ma) * (o_a @ w_br[:W_A])
              + jax.nn.sigmoid(mb) * (o_b @ w_br[W_A:W_A + W_B])
              + jax.nn.sigmoid(mc) * (o_c @ w_br[W_A + W_B:]))
    x = x + gt_m * (merged @ p['w_o'][l])

    h2 = rms_norm(x, p['g_ffn'][l]) * (1 + sc_f) + sh_f
    u_gate, u_up = jnp.split(h2 @ p['w_gu'][l], 2, axis=-1)
    x = x + gt_f * ((jax.nn.silu(u_gate) * u_up) @ p['w_down'][l])
    return x, k_c, v_c, s_a_new.astype(s_a.dtype), s_b_new.astype(s_b.dtype)


def run_trunk(x, c, p, lb, state_a, state_b, cache_k, cache_v, page_table):
    ks, vs, sas, sbs = [], [], [], []
    for l in range(DEPTH):
        if cache_k is None:
            k_past = None
            v_past = None
        else:
            db, n_pages = page_table.shape
            past = n_pages * cache_k.shape[2]
            k_past = cache_k[l][page_table].reshape(db, past, H_C, DH_C)
            v_past = cache_v[l][page_table].reshape(db, past, H_C, DH_C)
        x, k_new, v_new, sa, sb = layer(x, c, l, p, lb, state_a[l], state_b[l], k_past, v_past)
        ks.append(k_new)
        vs.append(v_new)
        sas.append(sa)
        sbs.append(sb)
    return rms_norm(x, p['g_final']), jnp.stack(ks), jnp.stack(vs), jnp.stack(sas), jnp.stack(sbs)


def setup_inputs(seed: int = 0) -> dict:
    key = jax.random.key(seed)
    ks = jax.random.split(key, 32)
    f32 = jnp.float32
    n_pages = PAST_LEN // PAGE_SIZE
    n_used = DEC_BATCH * n_pages
    n_pool = n_used + max(1, n_used // 4)

    def nrm(k, shape, scale):
        return jax.random.normal(k, shape, f32) * scale

    page_table = jax.random.permutation(ks[0], n_pool)[:n_used].reshape(DEC_BATCH, n_pages).astype(jnp.int32)
    w_br = jnp.concatenate([nrm(ks[20], (DEPTH, W_A, D_MODEL), W_A ** -0.5),
                            nrm(ks[21], (DEPTH, W_B, D_MODEL), W_B ** -0.5),
                            nrm(ks[22], (DEPTH, W_C, D_MODEL), W_C ** -0.5)], axis=1)
    return {
        'x_prompt': nrm(ks[1], (BATCH, SEQ, D_MODEL), 1.0),
        'x_sample': nrm(ks[2], (DEC_BATCH, DEC_SEQ, D_MODEL), 1.0),
        'cache_k': nrm(ks[3], (DEPTH, n_pool, PAGE_SIZE, H_C, DH_C), 1.0),
        'cache_v': nrm(ks[4], (DEPTH, n_pool, PAGE_SIZE, H_C, DH_C), 1.0),
        'state_hgrn': nrm(ks[5], (DEPTH, DEC_BATCH, H_A, DK_A, DV_A), 0.5),
        'state_gla': nrm(ks[6], (DEPTH, DEC_BATCH, H_B, DK_B, DV_B), 2.0),
        'page_table': page_table,
        'c_prompt': nrm(ks[7], (BATCH, D_MODEL), 1.0),
        'c_sample': nrm(ks[8], (DEC_BATCH, D_MODEL), 1.0),
        'w_ada': nrm(ks[9], (DEPTH, D_MODEL, N_MOD * D_MODEL), 0.5 * D_MODEL ** -0.5),
        'b_ada': nrm(ks[10], (DEPTH, N_MOD * D_MODEL), 0.01),
        'g_mix': 1.0 + nrm(ks[11], (DEPTH, D_MODEL), 0.01),
        'w_in': nrm(ks[12], (DEPTH, D_MODEL, N_IN), D_MODEL ** -0.5),
        'b_sb': SB_BIAS_INIT + nrm(ks[28], (DEPTH, H_C), 0.1),
        'lb_gamma': nrm(ks[13], (DEPTH, H_A * DK_A), 0.5),
        'gn_a': 1.0 + nrm(ks[14], (DEPTH, DV_A), 0.01),
        'w_gk': nrm(ks[15], (DEPTH, GK_RANK, H_B * DK_B), GK_RANK ** -0.5),
        'b_gk': nrm(ks[16], (DEPTH, H_B * DK_B), 0.1),
        'gn_b': 1.0 + nrm(ks[17], (DEPTH, DV_B), 0.01),
        'w_br': w_br,
        'w_o': nrm(ks[23], (DEPTH, D_MODEL, D_MODEL), D_MODEL ** -0.5),
        'g_ffn': 1.0 + nrm(ks[24], (DEPTH, D_MODEL), 0.01),
        'w_gu': nrm(ks[25], (DEPTH, D_MODEL, 2 * D_FF), D_MODEL ** -0.5),
        'w_down': nrm(ks[26], (DEPTH, D_FF, D_MODEL), D_FF ** -0.5),
        'g_final': 1.0 + nrm(ks[27], (D_MODEL,), 0.01),
    }


def reference(x_prompt, x_sample, cache_k, cache_v, state_hgrn, state_gla, page_table,
              c_prompt, c_sample, w_ada, b_ada, g_mix, w_in, b_sb, lb_gamma, gn_a, w_gk, b_gk,
              gn_b, w_br, w_o, g_ffn, w_gu, w_down, g_final):
    p = {'w_ada': w_ada, 'b_ada': b_ada, 'g_mix': g_mix, 'w_in': w_in, 'b_sb': b_sb,
         'gn_a': gn_a, 'w_gk': w_gk, 'b_gk': b_gk, 'gn_b': gn_b, 'w_br': w_br, 'w_o': w_o,
         'g_ffn': g_ffn, 'w_gu': w_gu, 'w_down': w_down, 'g_final': g_final}
    sm = jax.nn.softmax(lb_gamma.astype(jnp.float32), axis=0)
    cs = jnp.cumsum(sm, axis=0)
    lb = cs - cs[0:1]
    bp = x_prompt.shape[0]
    zeros_a = jnp.zeros((DEPTH, bp, H_A, DK_A, DV_A), state_hgrn.dtype)
    zeros_b = jnp.zeros((DEPTH, bp, H_B, DK_B, DV_B), state_gla.dtype)
    y_prompt, k_prompt, v_prompt, hgrn_prompt, gla_prompt = run_trunk(
        x_prompt, c_prompt, p, lb, zeros_a, zeros_b, None, None, None)
    y_sample, k_sample, v_sample, hgrn_sample, gla_sample = run_trunk(
        x_sample, c_sample, p, lb, state_hgrn, state_gla, cache_k, cache_v, page_table)
    return (y_prompt, y_sample, k_prompt, v_prompt, k_sample, v_sample,
            hgrn_prompt, hgrn_sample, gla_prompt, gla_sample)
```

```python
import functools
import math

import jax
import jax.numpy as jnp
from jax import lax
from jax.experimental import pallas as pl
from jax.experimental.pallas import tpu as pltpu

F32 = jnp.float32
BF16 = jnp.bfloat16

EPS = 1e-6
GK_NORM = 16.0
N_MOD = 6
LANES = 128
SUBLANES = 8
VMEM_LIMIT = 56 * 1024 * 1024


def _cparams(sem):
    return pltpu.CompilerParams(dimension_semantics=sem, vmem_limit_bytes=VMEM_LIMIT)


def _silu(x):
    return x * jax.nn.sigmoid(x)


def _log_sigmoid(x):
    return jnp.minimum(x, 0.0) - jnp.log1p(jnp.exp(-jnp.abs(x)))


def _logaddexp(a, b):
    m = jnp.maximum(a, b)
    return m + jnp.log1p(jnp.exp(-jnp.abs(a - b)))


def _dot(a, b):
    return jnp.dot(a, b, preferred_element_type=F32)


def _dot_nt(a, b):
    return lax.dot_general(a, b, (((1,), (1,)), ((), ())), preferred_element_type=F32)


def _dot_tn(a, b):
    return lax.dot_general(a, b, (((0,), (0,)), ((), ())), preferred_element_type=F32)


class _Path:
    def __init__(self, bv, lv, tm, per_row_mod, mod_row0):
        self.bv, self.lv, self.tm = bv, lv, tm
        self.per_row_mod = per_row_mod
        self.mod_row0 = mod_row0

    def grid_rows(self):
        return (self.bv, self.lv // self.tm)

    def mod_operand(self, mod_all):
        if self.per_row_mod:
            return mod_all
        d, r, w = mod_all.shape
        return mod_all.reshape(d, r, 1, w)

    def mod_spec(self, l, k, d_model, tn=None, with_j=False):
        tn = d_model if tn is None else tn
        per = d_model // tn
        if self.per_row_mod:
            if with_j:
                return pl.BlockSpec((None, self.tm, tn), lambda b, i, j: (l, i, k * per + j))
            return pl.BlockSpec((None, self.tm, tn), lambda b, i: (l, i, k * per))
        r0 = self.mod_row0
        if with_j:
            return pl.BlockSpec((None, None, 1, tn), lambda b, i, j: (l, r0 + b, 0, k * per + j))
        return pl.BlockSpec((None, None, 1, tn), lambda b, i: (l, r0 + b, 0, k * per))


def _ada_kernel(c_ref, w_ref, b_ref, o_ref):
    a = _silu(c_ref[...]).astype(BF16)
    o_ref[...] = _dot(a, w_ref[...].astype(BF16)) + b_ref[...]


def _ada_mod(c_all, w_ada, b_ada, tn=1024):
    depth, d, n = w_ada.shape
    rows = c_all.shape[0]
    return pl.pallas_call(
        _ada_kernel,
        grid=(depth, n // tn),
        in_specs=[
            pl.BlockSpec((rows, d), lambda l, j: (0, 0)),
            pl.BlockSpec((None, d, tn), lambda l, j: (l, 0, j)),
            pl.BlockSpec((None, 1, tn), lambda l, j: (l, 0, j)),
        ],
        out_specs=pl.BlockSpec((None, rows, tn), lambda l, j: (l, 0, j)),
        out_shape=jax.ShapeDtypeStruct((depth, rows, n), F32),
        compiler_params=_cparams(("parallel", "parallel")),
        name="ada_mod",
    )(c_all, w_ada, b_ada.reshape(depth, 1, n))


def _norm_mod_kernel(x_ref, g_ref, sc_ref, sh_ref, o_ref):
    x = x_ref[...]
    y = x * lax.rsqrt(jnp.mean(x * x, axis=-1, keepdims=True) + EPS) * g_ref[...]
    o_ref[...] = (y * (1.0 + sc_ref[...]) + sh_ref[...]).astype(o_ref.dtype)


def _norm_kernel(x_ref, g_ref, o_ref):
    x = x_ref[...]
    y = x * lax.rsqrt(jnp.mean(x * x, axis=-1, keepdims=True) + EPS) * g_ref[...]
    o_ref[...] = y.astype(o_ref.dtype)


def _norm_mod(path, x, g, mod_all, l, k_sc, k_sh, tm=None):
    bv, lv, d = x.shape
    tm = min(path.tm, 512) if tm is None else tm
    p = _Path(bv, lv, tm, path.per_row_mod, path.mod_row0)
    depth = g.shape[0]
    mod = p.mod_operand(mod_all)
    return pl.pallas_call(
        _norm_mod_kernel,
        grid=p.grid_rows(),
        in_specs=[
            pl.BlockSpec((None, tm, d), lambda b, i: (b, i, 0)),
            pl.BlockSpec((None, 1, d), lambda b, i: (l, 0, 0)),
            p.mod_spec(l, k_sc, d),
            p.mod_spec(l, k_sh, d),
        ],
        out_specs=pl.BlockSpec((None, tm, d), lambda b, i: (b, i, 0)),
        out_shape=jax.ShapeDtypeStruct((bv, lv, d), BF16),
        compiler_params=_cparams(("parallel", "parallel")),
        name="norm_mod",
    )(x, g.reshape(depth, 1, d), mod, mod)


def _final_norm(x, g, tm):
    bv, lv, d = x.shape
    return pl.pallas_call(
        _norm_kernel,
        grid=(bv, lv // tm),
        in_specs=[
            pl.BlockSpec((None, tm, d), lambda b, i: (b, i, 0)),
            pl.BlockSpec((1, d), lambda b, i: (0, 0)),
        ],
        out_specs=pl.BlockSpec((None, tm, d), lambda b, i: (b, i, 0)),
        out_shape=jax.ShapeDtypeStruct((bv, lv, d), F32),
        compiler_params=_cparams(("parallel", "parallel")),
        name="final_norm",
    )(x, g.reshape(1, d))


def _mm_kernel(a_ref, w_ref, o_ref):
    o_ref[...] = _dot(a_ref[...], w_ref[...]).astype(o_ref.dtype)


def _mm(path, a, w, l, col0, ncols, tn, tm=None, out_dtype=F32, name="mm"):
    bv, lv, k = a.shape
    tm = path.tm if tm is None else tm
    c0 = col0 // tn
    assert col0 % tn == 0 and ncols % tn == 0 and lv % tm == 0
    return pl.pallas_call(
        _mm_kernel,
        grid=(bv, lv // tm, ncols // tn),
        in_specs=[
            pl.BlockSpec((None, tm, k), lambda b, i, j: (b, i, 0)),
            pl.BlockSpec((None, k, tn), lambda b, i, j: (l, 0, c0 + j)),
        ],
        out_specs=pl.BlockSpec((None, tm, tn), lambda b, i, j: (b, i, j)),
        out_shape=jax.ShapeDtypeStruct((bv, lv, ncols), out_dtype),
        compiler_params=_cparams(("parallel", "parallel", "arbitrary")),
        name=name,
    )(a, w)


def _mm_resid_kernel(a_ref, w_ref, x_ref, gt_ref, o_ref):
    o_ref[...] = x_ref[...] + gt_ref[...] * _dot(a_ref[...], w_ref[...])


def _mm_resid(path, a, w, x, mod_all, l, k_gt, tn, tm=None, name="mm_resid"):
    bv, lv, k = a.shape
    d = w.shape[-1]
    tm = path.tm if tm is None else tm
    p = _Path(bv, lv, tm, path.per_row_mod, path.mod_row0)
    return pl.pallas_call(
        _mm_resid_kernel,
        grid=(bv, lv // tm, d // tn),
        in_specs=[
            pl.BlockSpec((None, tm, k), lambda b, i, j: (b, i, 0)),
            pl.BlockSpec((None, k, tn), lambda b, i, j: (l, 0, j)),
            pl.BlockSpec((None, tm, tn), lambda b, i, j: (b, i, j)),
            p.mod_spec(l, k_gt, d, tn=tn, with_j=True),
        ],
        out_specs=pl.BlockSpec((None, tm, tn), lambda b, i, j: (b, i, j)),
        out_shape=jax.ShapeDtypeStruct((bv, lv, d), F32),
        compiler_params=_cparams(("parallel", "parallel", "arbitrary")),
        name=name,
    )(a, w, x, p.mod_operand(mod_all))


def _swiglu_kernel(a_ref, wg_ref, wu_ref, o_ref):
    a = a_ref[...]
    o_ref[...] = (_silu(_dot(a, wg_ref[...])) * _dot(a, wu_ref[...])).astype(o_ref.dtype)


def _swiglu(path, a, w_gu, l, tn, tm=None):
    bv, lv, k = a.shape
    d_ff = w_gu.shape[-1] // 2
    tm = path.tm if tm is None else tm
    nj = d_ff // tn
    assert d_ff % tn == 0
    return pl.pallas_call(
        _swiglu_kernel,
        grid=(bv, lv // tm, nj),
        in_specs=[
            pl.BlockSpec((None, tm, k), lambda b, i, j: (b, i, 0)),
            pl.BlockSpec((None, k, tn), lambda b, i, j: (l, 0, j)),
            pl.BlockSpec((None, k, tn), lambda b, i, j: (l, 0, nj + j)),
        ],
        out_specs=pl.BlockSpec((None, tm, tn), lambda b, i, j: (b, i, j)),
        out_shape=jax.ShapeDtypeStruct((bv, lv, d_ff), BF16),
        compiler_params=_cparams(("parallel", "parallel", "arbitrary")),
        name="swiglu",
    )(a, w_gu, w_gu)


def _merge_kernel(oa_ref, ob_ref, oc_ref, ma_ref, mb_ref, mc_ref, wa_ref, wb_ref, wc_ref, o_ref):
    m = jax.nn.sigmoid(ma_ref[...]) * _dot(oa_ref[...], wa_ref[...])
    m = m + jax.nn.sigmoid(mb_ref[...]) * _dot(ob_ref[...], wb_ref[...])
    m = m + jax.nn.sigmoid(mc_ref[...]) * _dot(oc_ref[...], wc_ref[...])
    o_ref[...] = m.astype(o_ref.dtype)


def _merge(path, o_a, o_b, o_c, y, gate_col0, w_br, l, tn, tm=None):
    bv, lv, wa = o_a.shape
    wb, wc = o_b.shape[-1], o_c.shape[-1]
    d = w_br.shape[-1]
    tm = path.tm if tm is None else tm
    g0 = gate_col0 // tn
    per = d // tn
    assert gate_col0 % tn == 0 and wb == wa and wc == 2 * wa
    return pl.pallas_call(
        _merge_kernel,
        grid=(bv, lv // tm, d // tn),
        in_specs=[
            pl.BlockSpec((None, tm, wa), lambda b, i, j: (b, i, 0)),
            pl.BlockSpec((None, tm, wb), lambda b, i, j: (b, i, 0)),
            pl.BlockSpec((None, tm, wc), lambda b, i, j: (b, i, 0)),
            pl.BlockSpec((None, tm, tn), lambda b, i, j: (b, i, g0 + j)),
            pl.BlockSpec((None, tm, tn), lambda b, i, j: (b, i, g0 + per + j)),
            pl.BlockSpec((None, tm, tn), lambda b, i, j: (b, i, g0 + 2 * per + j)),
            pl.BlockSpec((None, wa, tn), lambda b, i, j: (l, 0, j)),
            pl.BlockSpec((None, wb, tn), lambda b, i, j: (l, 1, j)),
            pl.BlockSpec((None, wc, tn), lambda b, i, j: (l, 1, j)),
        ],
        out_specs=pl.BlockSpec((None, tm, tn), lambda b, i, j: (b, i, j)),
        out_shape=jax.ShapeDtypeStruct((bv, lv, d), BF16),
        compiler_params=_cparams(("parallel", "parallel", "arbitrary")),
        name="merge",
    )(o_a, o_b, o_c, y, y, y, w_br, w_br, w_br)


def _lb_from_gamma(gam, l):
    depth = len(gam)
    m = gam[0]
    for i in range(1, depth):
        m = jnp.maximum(m, gam[i])
    e = [jnp.exp(gam[i] - m) for i in range(depth)]
    tot = e[0]
    for i in range(1, depth):
        tot = tot + e[i]
    sm = [ei / tot for ei in e]
    cs = [sm[0]]
    for i in range(1, depth):
        cs.append(cs[-1] + sm[i])
    return cs[l] - cs[0]


def _rec_core(qq, kk, logf, v_ref_tile, g_tile, gn, st_ref, o_ref, *, chunk, n_groups, hpg):
    c = chunk
    wq = n_groups * LANES
    dk = LANES // hpg
    n_levels = int(math.log2(c))
    row = lax.broadcasted_iota(jnp.int32, (c, wq), 0)
    ti = lax.broadcasted_iota(jnp.int32, (c, c), 0)
    si = lax.broadcasted_iota(jnp.int32, (c, c), 1)
    x = jnp.bitwise_xor(ti, si)
    lvl = jnp.zeros((c, c), jnp.int32)
    for lev in range(1, n_levels + 1):
        lvl = lvl + jnp.where(x >= (1 << (lev - 1)), 1, 0)
    lvl = jnp.where(ti > si, lvl, -1)
    lane = lax.broadcasted_iota(jnp.int32, (c, LANES), 1)

    def head_mask(a, sub):
        if hpg == 1:
            return a
        return jnp.where(lane < dk, a, 0.0) if sub == 0 else jnp.where(lane >= dk, a, 0.0)

    pre = logf
    tot = logf
    ql, kl = [], []
    for lev in range(1, n_levels + 1):
        half = 1 << (lev - 1)
        upper = (row & half) != 0
        e = jnp.exp(jnp.where(upper, pre, tot - pre))
        ql.append(qq * e)
        kl.append(kk * e)
        up = pltpu.roll(tot, half, 0)
        dn = pltpu.roll(tot, c - half, 0)
        pre = pre + jnp.where(upper, up, 0.0)
        tot = tot + jnp.where(upper, up, dn)
    q_in = qq * jnp.exp(pre)
    k_out = kk * jnp.exp(tot - pre)
    d_all = jnp.exp(tot[0:1, :])
    qk = qq * kk

    for g in range(n_groups):
        gs = slice(g * LANES, (g + 1) * LANES)
        st = st_ref[g]
        st_bf = st.astype(BF16)
        st_new = st * d_all[:, gs]
        for sub in range(hpg):
            h = g * hpg + sub
            hs = slice(h * LANES, (h + 1) * LANES)
            v_bf = v_ref_tile[:, hs].astype(BF16)
            att = jnp.zeros((c, c), F32)
            for lev in range(1, n_levels + 1):
                p = _dot_nt(head_mask(ql[lev - 1][:, gs], sub).astype(BF16), kl[lev - 1][:, gs].astype(BF16))
                att = jnp.where(lvl == lev, p, att)
            diag = jnp.sum(head_mask(qk[:, gs], sub), axis=-1, keepdims=True)
            att = jnp.where(x == 0, diag, att)
            o = _dot_nt(head_mask(q_in[:, gs], sub).astype(BF16), st_bf) + _dot(att.astype(BF16), v_bf)
            st_new = st_new + _dot_tn(v_bf, head_mask(k_out[:, gs], sub).astype(BF16))
            y = o * lax.rsqrt(jnp.mean(o * o, axis=-1, keepdims=True) + EPS) * gn
            o_ref[:, hs] = (y * _silu(g_tile[:, hs])).astype(o_ref.dtype)
        st_ref[g] = st_new


def _state_in(s0_ref, st_ref, n_groups, hpg):
    dk = LANES // hpg
    for g in range(n_groups):
        blk = jnp.concatenate([s0_ref[g * hpg + sub] for sub in range(hpg)], axis=0) if hpg > 1 else s0_ref[g]
        st_ref[g] = blk.T


def _state_out(st_ref, sout_ref, n_groups, hpg):
    dk = LANES // hpg
    for g in range(n_groups):
        t = st_ref[g].T
        for sub in range(hpg):
            sout_ref[g * hpg + sub] = t[sub * dk:(sub + 1) * dk, :]


def _rec_a_kernel(q_ref, f_ref, i_ref, g_ref, lbg_ref, gn_ref, s0_ref, o_ref, sout_ref, st_ref, *, l, chunk):
    ci = pl.program_id(1)

    @pl.when(ci == 0)
    def _():
        _state_in(s0_ref, st_ref, 4, 1)

    lb = _lb_from_gamma([lbg_ref[i:i + 1, :] for i in range(lbg_ref.shape[0])], l)
    f = f_ref[...]
    logf = _logaddexp(jnp.log(lb), jnp.log1p(-lb) + _log_sigmoid(f))
    kk = (1.0 - lb) * jax.nn.sigmoid(-f)
    qq = _silu(q_ref[...]) * (LANES ** -0.5)
    _rec_core(qq, kk, logf, i_ref, g_ref[...], gn_ref[...], st_ref, o_ref, chunk=chunk, n_groups=4, hpg=1)

    @pl.when(ci == pl.num_programs(1) - 1)
    def _():
        _state_out(st_ref, sout_ref, 4, 1)


def _rec_b_kernel(q_ref, k_ref, v_ref, g_ref, r_ref, wgk_ref, bgk_ref, gn_ref, s0_ref, o_ref, sout_ref, st_ref,
                  *, chunk):
    ci = pl.program_id(1)

    @pl.when(ci == 0)
    def _():
        _state_in(s0_ref, st_ref, 2, 2)

    gk = _dot(r_ref[...].astype(BF16), wgk_ref[...]) + bgk_ref[...]
    logf = _log_sigmoid(gk) * (1.0 / GK_NORM)
    qq = q_ref[...] * ((LANES // 2) ** -0.5)
    _rec_core(qq, k_ref[...], logf, v_ref, g_ref[...], gn_ref[...], st_ref, o_ref, chunk=chunk, n_groups=2, hpg=2)

    @pl.when(ci == pl.num_programs(1) - 1)
    def _():
        _state_out(st_ref, sout_ref, 2, 2)


def _rec_a(y, lb_gamma, gn_a, s0, l, chunk):
    b, L, _ = y.shape
    depth = lb_gamma.shape[0]
    w = 4 * LANES
    col = lambda k: pl.BlockSpec((None, chunk, w), lambda bi, ci: (bi, ci, k))
    return pl.pallas_call(
        functools.partial(_rec_a_kernel, l=l, chunk=chunk),
        grid=(b, L // chunk),
        in_specs=[
            col(0), col(1), col(2), col(3),
            pl.BlockSpec((depth, w), lambda bi, ci: (0, 0)),
            pl.BlockSpec((None, 1, LANES), lambda bi, ci: (l, 0, 0)),
            pl.BlockSpec((None, 4, LANES, LANES), lambda bi, ci: (bi, 0, 0, 0)),
        ],
        out_specs=[
            pl.BlockSpec((None, chunk, w), lambda bi, ci: (bi, ci, 0)),
            pl.BlockSpec((None, 4, LANES, LANES), lambda bi, ci: (bi, 0, 0, 0)),
        ],
        out_shape=[jax.ShapeDtypeStruct((b, L, w), BF16), jax.ShapeDtypeStruct(s0.shape, F32)],
        scratch_shapes=[pltpu.VMEM((4, LANES, LANES), F32)],
        compiler_params=_cparams(("parallel", "arbitrary")),
        name="rec_hgrn",
    )(y, y, y, y, lb_gamma, gn_a.reshape(depth, 1, LANES), s0)


def _rec_b(y, r, wgk_p, b_gk, gn_b, s0, l, chunk, col_b):
    b, L, _ = y.shape
    depth = gn_b.shape[0]
    c256 = col_b // 256
    c512 = (col_b + 512) // 512
    return pl.pallas_call(
        functools.partial(_rec_b_kernel, chunk=chunk),
        grid=(b, L // chunk),
        in_specs=[
            pl.BlockSpec((None, chunk, 256), lambda bi, ci: (bi, ci, c256)),
            pl.BlockSpec((None, chunk, 256), lambda bi, ci: (bi, ci, c256 + 1)),
            pl.BlockSpec((None, chunk, 512), lambda bi, ci: (bi, ci, c512)),
            pl.BlockSpec((None, chunk, 512), lambda bi, ci: (bi, ci, c512 + 1)),
            pl.BlockSpec((None, chunk, LANES), lambda bi, ci: (bi, ci, 0)),
            pl.BlockSpec((None, LANES, 256), lambda bi, ci: (l, 0, 0)),
            pl.BlockSpec((None, 1, 256), lambda bi, ci: (l, 0, 0)),
            pl.BlockSpec((None, 1, LANES), lambda bi, ci: (l, 0, 0)),
            pl.BlockSpec((None, 4, 64, LANES), lambda bi, ci: (bi, 0, 0, 0)),
        ],
        out_specs=[
            pl.BlockSpec((None, chunk, 512), lambda bi, ci: (bi, ci, 0)),
            pl.BlockSpec((None, 4, 64, LANES), lambda bi, ci: (bi, 0, 0, 0)),
        ],
        out_shape=[jax.ShapeDtypeStruct((b, L, 512), BF16), jax.ShapeDtypeStruct(s0.shape, F32)],
        scratch_shapes=[pltpu.VMEM((2, LANES, LANES), F32)],
        compiler_params=_cparams(("parallel", "arbitrary")),
        name="rec_gla",
    )(y, y, y, y, r, wgk_p, b_gk.reshape(depth, 1, 256), gn_b.reshape(depth, 1, LANES), s0)


def _dec_finish(s_new, q_col, g_row, gn, o_ref, h):
    o = jnp.sum(q_col * s_new, axis=0, keepdims=True)
    y = o * lax.rsqrt(jnp.mean(o * o, axis=-1, keepdims=True) + EPS) * gn
    o_ref[h] = y * _silu(g_row)


def _dec_a_kernel(q_ref, f_ref, i_ref, g_ref, lbg_ref, gn_ref, s0_ref, o_ref, sout_ref, *, l):
    lb_all = _lb_from_gamma([lbg_ref[i] for i in range(lbg_ref.shape[0])], l)
    for h in range(4):
        lb = lb_all[h]
        f = f_ref[h]
        logf = _logaddexp(jnp.log(lb), jnp.log1p(-lb) + _log_sigmoid(f))
        kk = (1.0 - lb) * jax.nn.sigmoid(-f)
        qq = _silu(q_ref[h]) * (LANES ** -0.5)
        s_new = jnp.exp(logf) * s0_ref[h] + kk * i_ref[h]
        sout_ref[h] = s_new
        _dec_finish(s_new, qq, g_ref[h], gn_ref[...], o_ref, h)


def _dec_b_kernel(q_ref, k_ref, v_ref, g_ref, r_ref, wgk_ref, bgk_ref, gn_ref, s0_ref, o_ref, sout_ref):
    r = r_ref[...]
    for h in range(4):
        gk = jnp.sum(wgk_ref[h] * r, axis=-1, keepdims=True) + bgk_ref[h]
        logf = _log_sigmoid(gk) * (1.0 / GK_NORM)
        qq = q_ref[h] * ((LANES // 2) ** -0.5)
        s_new = jnp.exp(logf) * s0_ref[h] + k_ref[h] * v_ref[h]
        sout_ref[h] = s_new
        _dec_finish(s_new, qq, g_ref[h], gn_ref[...], o_ref, h)


def _dec_a(qa, fa, ia, ga, lb_gamma, gn_a, s0, l):
    nb = qa.shape[0]
    depth = lb_gamma.shape[0]
    colspec = pl.BlockSpec((None, 4, LANES, 1), lambda b: (b, 0, 0, 0))
    rowspec = pl.BlockSpec((None, 4, 1, LANES), lambda b: (b, 0, 0, 0))
    stspec = pl.BlockSpec((None, 4, LANES, LANES), lambda b: (b, 0, 0, 0))
    o, s = pl.pallas_call(
        functools.partial(_dec_a_kernel, l=l),
        grid=(nb,),
        in_specs=[colspec, colspec, rowspec, rowspec,
                  pl.BlockSpec((depth, 4, LANES, 1), lambda b: (0, 0, 0, 0)),
                  pl.BlockSpec((None, 1, LANES), lambda b: (l, 0, 0)),
                  stspec],
        out_specs=[rowspec, stspec],
        out_shape=[jax.ShapeDtypeStruct((nb, 4, 1, LANES), F32), jax.ShapeDtypeStruct(s0.shape, F32)],
        compiler_params=_cparams(("parallel",)),
        name="dec_hgrn",
    )(qa.reshape(nb, 4, LANES, 1), fa.reshape(nb, 4, LANES, 1), ia.reshape(nb, 4, 1, LANES),
      ga.reshape(nb, 4, 1, LANES), lb_gamma.reshape(depth, 4, LANES, 1), gn_a.reshape(depth, 1, LANES), s0)
    return o.reshape(nb, 1, 4 * LANES).astype(BF16), s


def _dec_b(qb, kb, vb, gb, rb, wgk_t, b_gk, gn_b, s0, l):
    nb = qb.shape[0]
    depth = gn_b.shape[0]
    rank = rb.shape[-1]
    colspec = pl.BlockSpec((None, 4, 64, 1), lambda b: (b, 0, 0, 0))
    rowspec = pl.BlockSpec((None, 4, 1, LANES), lambda b: (b, 0, 0, 0))
    stspec = pl.BlockSpec((None, 4, 64, LANES), lambda b: (b, 0, 0, 0))
    o, s = pl.pallas_call(
        _dec_b_kernel,
        grid=(nb,),
        in_specs=[colspec, colspec, rowspec, rowspec,
                  pl.BlockSpec((None, 1, rank), lambda b: (b, 0, 0)),
                  pl.BlockSpec((None, 4, 64, rank), lambda b: (l, 0, 0, 0)),
                  pl.BlockSpec((None, 4, 64, 1), lambda b: (l, 0, 0, 0)),
                  pl.BlockSpec((None, 1, LANES), lambda b: (l, 0, 0)),
                  stspec],
        out_specs=[rowspec, stspec],
        out_shape=[jax.ShapeDtypeStruct((nb, 4, 1, LANES), F32), jax.ShapeDtypeStruct(s0.shape, F32)],
        compiler_params=_cparams(("parallel",)),
        name="dec_gla",
    )(qb.reshape(nb, 4, 64, 1), kb.reshape(nb, 4, 64, 1), vb.reshape(nb, 4, 1, LANES),
      gb.reshape(nb, 4, 1, LANES), rb.reshape(nb, 1, rank), wgk_t, b_gk.reshape(depth, 4, 64, 1),
      gn_b.reshape(depth, 1, LANES), s0)
    return o.reshape(nb, 1, 4 * LANES).astype(BF16), s


def _suffix_matrix(n, rows_first):
    a = lax.broadcasted_iota(jnp.int32, (n, n), 0)
    b = lax.broadcasted_iota(jnp.int32, (n, n), 1)
    if rows_first:
        m = jnp.where(a >= b, 1.0, 0.0).astype(BF16)
        return jnp.concatenate([m, m], axis=0)
    m = jnp.where(b >= a, 1.0, 0.0).astype(BF16)
    return jnp.concatenate([m, m], axis=1)


def _sb_prompt_kernel(bias_ref, q_ref, k_ref, v_ref, o_ref, kbf_ref, vbf_ref, *, l, tq, scale):
    h = pl.program_id(1)
    qi = pl.program_id(2)

    @pl.when(qi == 0)
    def _():
        kbf_ref[...] = k_ref[...].astype(BF16)
        vbf_ref[...] = v_ref[...].astype(BF16)

    bias = bias_ref[l, h]
    q = q_ref[...].astype(BF16)
    uu = _suffix_matrix(tq, rows_first=True)
    ti = lax.broadcasted_iota(jnp.int32, (tq, tq), 0)
    si = lax.broadcasted_iota(jnp.int32, (tq, tq), 1)
    causal = si < ti

    def block(kb, run, acc, masked):
        start = pl.multiple_of(kb * tq, tq)
        ks = kbf_ref[pl.ds(start, tq), :]
        vs = vbf_ref[pl.ds(start, tq), :]
        z = _dot_nt(q, ks) * scale + bias
        lk_raw = -(jnp.maximum(z, 0.0) + jnp.log1p(jnp.exp(-jnp.abs(z))))
        lk = jnp.where(causal, lk_raw, 0.0) if masked else lk_raw
        hi = lk.astype(BF16)
        lo = (lk - hi.astype(F32)).astype(BF16)
        suf = _dot(jnp.concatenate([hi, lo], axis=1), uu)
        w = jnp.exp(z + lk_raw + (suf - lk) + run)
        if masked:
            w = jnp.where(causal, w, 0.0)
        acc = acc + _dot(w.astype(BF16), vs)
        return run + suf[:, 0:1], acc

    run0 = jnp.zeros((tq, 1), F32)
    acc0 = jnp.zeros((tq, LANES), F32)
    run1, acc1 = block(qi, run0, acc0, True)

    def body(j, carry):
        return block(qi - 1 - j, carry[0], carry[1], False)

    _, acc2 = lax.fori_loop(0, qi, body, (run1, acc1))
    o_ref[...] = acc2.astype(o_ref.dtype)


def _sb_prompt(y, b_sb, l, col_q, n_heads, tq):
    b, L, _ = y.shape
    cq = col_q // LANES
    scale = LANES ** -0.5
    return pl.pallas_call(
        functools.partial(_sb_prompt_kernel, l=l, tq=tq, scale=scale),
        grid=(b, n_heads, L // tq),
        in_specs=[
            pl.BlockSpec(memory_space=pltpu.SMEM),
            pl.BlockSpec((None, tq, LANES), lambda bi, h, qi: (bi, qi, cq + h)),
            pl.BlockSpec((None, L, LANES), lambda bi, h, qi: (bi, 0, cq + n_heads + h)),
            pl.BlockSpec((None, L, LANES), lambda bi, h, qi: (bi, 0, cq + 2 * n_heads + h)),
        ],
        out_specs=pl.BlockSpec((None, tq, LANES), lambda bi, h, qi: (bi, qi, h)),
        out_shape=jax.ShapeDtypeStruct((b, L, n_heads * LANES), BF16),
        scratch_shapes=[pltpu.VMEM((L, LANES), BF16), pltpu.VMEM((L, LANES), BF16)],
        compiler_params=_cparams(("parallel", "parallel", "arbitrary")),
        name="sb_prompt",
    )(b_sb, y, y, y)


def _sb_decode_kernel(pt_ref, bias_ref, q_ref, k_ref, v_ref, o_ref, qblk_ref, run_ref, acc_ref,
                      *, n_heads, page, scale):
    p = pl.program_id(1)
    hd = n_heads * LANES

    @pl.when(p == 0)
    def _():
        rh = lax.broadcasted_iota(jnp.int32, (hd, LANES), 0) // LANES
        ln = lax.broadcasted_iota(jnp.int32, (hd, LANES), 1)
        qblk_ref[...] = jnp.where(rh == ln, q_ref[...], 0.0).astype(BF16)
        run_ref[...] = jnp.zeros_like(run_ref)
        acc_ref[...] = jnp.zeros_like(acc_ref)

    z = _dot(k_ref[...].astype(BF16), qblk_ref[...]) * scale + bias_ref[...]
    lk = -(jnp.maximum(z, 0.0) + jnp.log1p(jnp.exp(-jnp.abs(z))))
    hi = lk.astype(BF16)
    lo = (lk - hi.astype(F32)).astype(BF16)
    suf = _dot(_suffix_matrix(page, rows_first=False), jnp.concatenate([hi, lo], axis=0))
    w = jnp.exp(z + suf + run_ref[...])
    wt = w.T[0:SUBLANES, :].astype(BF16)
    acc_ref[...] += _dot(wt, v_ref[...].astype(BF16))
    run_ref[...] += suf[0:1, :]

    @pl.when(p == pl.num_programs(1) - 1)
    def _():
        for h in range(n_heads):
            o_ref[h:h + 1, :] = acc_ref[h:h + 1, h * LANES:(h + 1) * LANES]


def _sb_decode(q, cache_k, cache_v, page_table, b_sb, l, n_heads):
    nb = q.shape[0]
    _, _, page, hd = cache_k.shape
    n_pages = page_table.shape[1]
    assert n_heads == SUBLANES and hd == n_heads * LANES
    bias_row = jnp.zeros((1, LANES), F32).at[0, :n_heads].set(b_sb[l])
    grid_spec = pltpu.PrefetchScalarGridSpec(
        num_scalar_prefetch=1,
        grid=(nb, n_pages),
        in_specs=[
            pl.BlockSpec((1, LANES), lambda b, p, pt: (0, 0)),
            pl.BlockSpec((None, hd, 1), lambda b, p, pt: (b, 0, 0)),
            pl.BlockSpec((None, None, page, hd), lambda b, p, pt: (l, pt[b, n_pages - 1 - p], 0, 0)),
            pl.BlockSpec((None, None, page, hd), lambda b, p, pt: (l, pt[b, n_pages - 1 - p], 0, 0)),
        ],
        out_specs=pl.BlockSpec((None, n_heads, LANES), lambda b, p, pt: (b, 0, 0)),
        scratch_shapes=[pltpu.VMEM((hd, LANES), BF16), pltpu.VMEM((1, LANES), F32),
                        pltpu.VMEM((SUBLANES, hd), F32)],
    )
    o = pl.pallas_call(
        functools.partial(_sb_decode_kernel, n_heads=n_heads, page=page, scale=LANES ** -0.5),
        grid_spec=grid_spec,
        out_shape=jax.ShapeDtypeStruct((nb, n_heads, LANES), F32),
        compiler_params=_cparams(("parallel", "arbitrary")),
        name="sb_decode",
    )(page_table, bias_row, q.reshape(nb, hd, 1), cache_k, cache_v)
    return o.reshape(nb, 1, hd).astype(BF16)


def _trunk(path, x, mod_all, wts, state_a, state_b, cache, dims):
    d_model = x.shape[-1]
    depth = wts["w_in"].shape[0]
    col = dims["col"]
    is_prompt = cache is None
    tm = path.tm
    ks, vs, sas, sbs = [], [], [], []
    for l in range(depth):
        h = _norm_mod(path, x, wts["g_mix"], mod_all, l, 1, 0)
        y = _mm(path, h, wts["w_in"], l, 0, col["rb"], 512, name="in_proj")
        r = _mm(path, h, wts["w_in"], l, col["rb"], LANES, LANES, name="in_proj_r")
        bv, lv, _ = y.shape
        k_new = y[:, :, col["kc"]:col["vc"]]
        v_new = y[:, :, col["vc"]:col["ma"]]
        if is_prompt:
            o_a, s_a = _rec_a(y, wts["lb_gamma"], wts["gn_a"], state_a[l], l, dims["chunk"])
            o_b, s_b = _rec_b(y, r, wts["w_gk_p"], wts["b_gk"], wts["gn_b"], state_b[l], l, dims["chunk"], col["qb"])
            o_c = _sb_prompt(y, wts["b_sb"], l, col["qc"], dims["h_c"], dims["tq"])
        else:
            y2 = y.reshape(lv, -1)
            sl = lambda a, b_: y2[:, col[a]:col[b_]]
            o_a, s_a = _dec_a(sl("qa", "fa"), sl("fa", "ia"), sl("ia", "ga"), sl("ga", "qb"),
                              wts["lb_gamma"], wts["gn_a"], state_a[l], l)
            o_b, s_b = _dec_b(sl("qb", "kb"), sl("kb", "vb"), sl("vb", "gb"), sl("gb", "qc"),
                              r.reshape(lv, -1)[:, :dims["gk_rank"]], wts["w_gk_t"], wts["b_gk"], wts["gn_b"],
                              state_b[l], l)
            o_c = _sb_decode(sl("qc", "kc"), cache[0], cache[1], cache[2], wts["b_sb"], l, dims["h_c"])
            o_a, o_b, o_c = (o.reshape(bv, lv, -1) for o in (o_a, o_b, o_c))
        merged = _merge(path, o_a, o_b, o_c, y, col["ma"], wts["w_br"], l, 512)
        x = _mm_resid(path, merged, wts["w_o"], x, mod_all, l, 2, 512, name="out_proj")
        h2 = _norm_mod(path, x, wts["g_ffn"], mod_all, l, 4, 3)
        act = _swiglu(path, h2, wts["w_gu"], l, 512)
        x = _mm_resid(path, act, wts["w_down"], x, mod_all, l, 5, 512, tm=min(tm, 512), name="down_proj")
        ks.append(k_new)
        vs.append(v_new)
        sas.append(s_a)
        sbs.append(s_b)
    xf = _final_norm(x, wts["g_final"], min(tm, 512))
    return xf, jnp.stack(ks), jnp.stack(vs), jnp.stack(sas), jnp.stack(sbs)


def kernel(x_prompt, x_sample, cache_k, cache_v, state_hgrn, state_gla, page_table, c_prompt, c_sample, w_ada, b_ada, g_mix, w_in, b_sb, lb_gamma, gn_a, w_gk, b_gk, gn_b, w_br, w_o, g_ffn, w_gu, w_down, g_final):
    bp, seq, d_model = x_prompt.shape
    nb = x_sample.shape[0]
    depth, _, _, h_c, dh_c = cache_k.shape[0], None, None, cache_k.shape[3], cache_k.shape[4]
    h_a, dk_a, dv_a = state_hgrn.shape[2:]
    h_b, dk_b, dv_b = state_gla.shape[2:]
    gk_rank = w_gk.shape[1]
    w_a, w_b, w_c = h_a * dv_a, h_b * dv_b, h_c * dh_c

    widths = [("qa", h_a * dk_a), ("fa", h_a * dk_a), ("ia", w_a), ("ga", w_a),
              ("qb", h_b * dk_b), ("kb", h_b * dk_b), ("vb", w_b), ("gb", w_b),
              ("qc", w_c), ("kc", w_c), ("vc", w_c),
              ("ma", d_model), ("mb", d_model), ("mc", d_model)]
    col, off = {}, 0
    for name, wd in widths:
        col[name] = off
        off += wd
    col["rb"] = off
    src_rb = 2 * h_a * dk_a + 2 * w_a + 2 * h_b * dk_b + 2 * w_b
    w_in_r = jnp.concatenate(
        [w_in[:, :, :src_rb], w_in[:, :, src_rb + gk_rank:], w_in[:, :, src_rb:src_rb + gk_rank],
         jnp.zeros((depth, d_model, LANES - gk_rank), w_in.dtype)], axis=-1).astype(BF16)
    w_gk_p = jnp.concatenate([w_gk, jnp.zeros((depth, LANES - gk_rank, w_gk.shape[-1]), w_gk.dtype)],
                             axis=1).astype(BF16)
    wts = {
        "w_in": w_in_r, "w_br": w_br.astype(BF16), "w_o": w_o.astype(BF16), "w_gu": w_gu.astype(BF16),
        "w_down": w_down.astype(BF16), "w_gk_p": w_gk_p,
        "w_gk_t": jnp.swapaxes(w_gk, 1, 2).reshape(depth, h_b, dk_b, gk_rank),
        "g_mix": g_mix, "g_ffn": g_ffn, "g_final": g_final, "b_sb": b_sb, "lb_gamma": lb_gamma,
        "gn_a": gn_a, "gn_b": gn_b, "b_gk": b_gk,
    }
    dims = {"col": col, "chunk": 256, "tq": 256, "h_c": h_c, "gk_rank": gk_rank}

    c_all = jnp.concatenate([c_sample, c_prompt, jnp.zeros((16 - nb - bp, d_model), F32)], axis=0)
    mod_all = _ada_mod(c_all, w_ada, b_ada)

    prompt = _Path(bp, seq, min(1024, seq), per_row_mod=False, mod_row0=nb)
    sample = _Path(1, nb, nb, per_row_mod=True, mod_row0=0)

    zeros_a = jnp.zeros((depth, bp) + state_hgrn.shape[2:], state_hgrn.dtype)
    zeros_b = jnp.zeros((depth, bp) + state_gla.shape[2:], state_gla.dtype)
    y_p, k_p, v_p, sa_p, sb_p = _trunk(prompt, x_prompt, mod_all, wts, zeros_a, zeros_b, None, dims)

    n_pool, page = cache_k.shape[1], cache_k.shape[2]
    cache = (cache_k.reshape(depth, n_pool, page, w_c), cache_v.reshape(depth, n_pool, page, w_c), page_table)
    y_s, k_s, v_s, sa_s, sb_s = _trunk(sample, x_sample.reshape(1, nb, d_model), mod_all, wts,
                                       state_hgrn, state_gla, cache, dims)

    return (y_p, y_s.reshape(nb, 1, d_model),
            k_p.reshape(depth, bp, seq, h_c, dh_c), v_p.reshape(depth, bp, seq, h_c, dh_c),
            k_s.reshape(depth, nb, 1, h_c, dh_c), v_s.reshape(depth, nb, 1, h_c, dh_c),
            sa_p, sa_s, sb_p, sb_s)
```

```python
import functools
import math

import jax
import jax.numpy as jnp
from jax import lax
from jax.experimental import pallas as pl
from jax.experimental.pallas import tpu as pltpu

F32 = jnp.float32
BF16 = jnp.bfloat16

EPS = 1e-6
GK_NORM = 16.0
N_MOD = 6
LANES = 128
SUBLANES = 8
VMEM_LIMIT = 56 * 1024 * 1024
SB_UNROLL = 4
DEC_PAGES = 8


def _cparams(sem):
    return pltpu.CompilerParams(dimension_semantics=sem, vmem_limit_bytes=VMEM_LIMIT)


def _silu(x):
    return x * jax.nn.sigmoid(x)


def _log_sigmoid(x):
    return jnp.minimum(x, 0.0) - jnp.log1p(jnp.exp(-jnp.abs(x)))


def _logaddexp(a, b):
    m = jnp.maximum(a, b)
    return m + jnp.log1p(jnp.exp(-jnp.abs(a - b)))


def _dot(a, b):
    return jnp.dot(a, b, preferred_element_type=F32)


def _dot_nt(a, b):
    return lax.dot_general(a, b, (((1,), (1,)), ((), ())), preferred_element_type=F32)


def _dot_tn(a, b):
    return lax.dot_general(a, b, (((0,), (0,)), ((), ())), preferred_element_type=F32)


class _Path:
    def __init__(self, bv, lv, tm, per_row_mod, mod_row0):
        self.bv, self.lv, self.tm = bv, lv, tm
        self.per_row_mod = per_row_mod
        self.mod_row0 = mod_row0

    def grid_rows(self):
        return (self.bv, self.lv // self.tm)

    def mod_operand(self, mod_all):
        if self.per_row_mod:
            return mod_all
        d, r, w = mod_all.shape
        return mod_all.reshape(d, r, 1, w)

    def mod_spec(self, l, k, d_model, tn=None, with_j=False):
        tn = d_model if tn is None else tn
        per = d_model // tn
        if self.per_row_mod:
            if with_j:
                return pl.BlockSpec((None, self.tm, tn), lambda b, i, j: (l, i, k * per + j))
            return pl.BlockSpec((None, self.tm, tn), lambda b, i: (l, i, k * per))
        r0 = self.mod_row0
        if with_j:
            return pl.BlockSpec((None, None, 1, tn), lambda b, i, j: (l, r0 + b, 0, k * per + j))
        return pl.BlockSpec((None, None, 1, tn), lambda b, i: (l, r0 + b, 0, k * per))


def _ada_kernel(c_ref, w_ref, b_ref, o_ref):
    a = _silu(c_ref[...]).astype(BF16)
    o_ref[...] = _dot(a, w_ref[...].astype(BF16)) + b_ref[...]


def _ada_mod(c_all, w_ada, b_ada, tn=1024):
    depth, d, n = w_ada.shape
    rows = c_all.shape[0]
    return pl.pallas_call(
        _ada_kernel,
        grid=(depth, n // tn),
        in_specs=[
            pl.BlockSpec((rows, d), lambda l, j: (0, 0)),
            pl.BlockSpec((None, d, tn), lambda l, j: (l, 0, j)),
            pl.BlockSpec((None, 1, tn), lambda l, j: (l, 0, j)),
        ],
        out_specs=pl.BlockSpec((None, rows, tn), lambda l, j: (l, 0, j)),
        out_shape=jax.ShapeDtypeStruct((depth, rows, n), F32),
        compiler_params=_cparams(("parallel", "parallel")),
        name="ada_mod",
    )(c_all, w_ada, b_ada.reshape(depth, 1, n))


def _norm_mod_kernel(x_ref, g_ref, sc_ref, sh_ref, o_ref):
    x = x_ref[...]
    y = x * lax.rsqrt(jnp.mean(x * x, axis=-1, keepdims=True) + EPS) * g_ref[...]
    o_ref[...] = (y * (1.0 + sc_ref[...]) + sh_ref[...]).astype(o_ref.dtype)


def _norm_kernel(x_ref, g_ref, o_ref):
    x = x_ref[...]
    y = x * lax.rsqrt(jnp.mean(x * x, axis=-1, keepdims=True) + EPS) * g_ref[...]
    o_ref[...] = y.astype(o_ref.dtype)


def _norm_mod(path, x, g, mod_all, l, k_sc, k_sh, tm=None):
    bv, lv, d = x.shape
    tm = min(path.tm, 512) if tm is None else tm
    p = _Path(bv, lv, tm, path.per_row_mod, path.mod_row0)
    depth = g.shape[0]
    mod = p.mod_operand(mod_all)
    return pl.pallas_call(
        _norm_mod_kernel,
        grid=p.grid_rows(),
        in_specs=[
            pl.BlockSpec((None, tm, d), lambda b, i: (b, i, 0)),
            pl.BlockSpec((None, 1, d), lambda b, i: (l, 0, 0)),
            p.mod_spec(l, k_sc, d),
            p.mod_spec(l, k_sh, d),
        ],
        out_specs=pl.BlockSpec((None, tm, d), lambda b, i: (b, i, 0)),
        out_shape=jax.ShapeDtypeStruct((bv, lv, d), BF16),
        compiler_params=_cparams(("parallel", "parallel")),
        name="norm_mod",
    )(x, g.reshape(depth, 1, d), mod, mod)


def _final_norm(x, g, tm):
    bv, lv, d = x.shape
    return pl.pallas_call(
        _norm_kernel,
        grid=(bv, lv // tm),
        in_specs=[
            pl.BlockSpec((None, tm, d), lambda b, i: (b, i, 0)),
            pl.BlockSpec((1, d), lambda b, i: (0, 0)),
        ],
        out_specs=pl.BlockSpec((None, tm, d), lambda b, i: (b, i, 0)),
        out_shape=jax.ShapeDtypeStruct((bv, lv, d), F32),
        compiler_params=_cparams(("parallel", "parallel")),
        name="final_norm",
    )(x, g.reshape(1, d))


def _mm_kernel(a_ref, w_ref, o_ref):
    o_ref[...] = _dot(a_ref[...], w_ref[...]).astype(o_ref.dtype)


def _mm(path, a, w, l, col0, ncols, tn, tm=None, out_dtype=F32, name="mm"):
    bv, lv, k = a.shape
    tm = path.tm if tm is None else tm
    c0 = col0 // tn
    assert col0 % tn == 0 and ncols % tn == 0 and lv % tm == 0
    return pl.pallas_call(
        _mm_kernel,
        grid=(bv, lv // tm, ncols // tn),
        in_specs=[
            pl.BlockSpec((None, tm, k), lambda b, i, j: (b, i, 0)),
            pl.BlockSpec((None, k, tn), lambda b, i, j: (l, 0, c0 + j)),
        ],
        out_specs=pl.BlockSpec((None, tm, tn), lambda b, i, j: (b, i, j)),
        out_shape=jax.ShapeDtypeStruct((bv, lv, ncols), out_dtype),
        compiler_params=_cparams(("parallel", "parallel", "arbitrary")),
        name=name,
    )(a, w)


def _mm_resid_kernel(a_ref, w_ref, x_ref, gt_ref, o_ref):
    o_ref[...] = x_ref[...] + gt_ref[...] * _dot(a_ref[...], w_ref[...])


def _mm_resid(path, a, w, x, mod_all, l, k_gt, tn, tm=None, name="mm_resid"):
    bv, lv, k = a.shape
    d = w.shape[-1]
    tm = path.tm if tm is None else tm
    p = _Path(bv, lv, tm, path.per_row_mod, path.mod_row0)
    return pl.pallas_call(
        _mm_resid_kernel,
        grid=(bv, lv // tm, d // tn),
        in_specs=[
            pl.BlockSpec((None, tm, k), lambda b, i, j: (b, i, 0)),
            pl.BlockSpec((None, k, tn), lambda b, i, j: (l, 0, j)),
            pl.BlockSpec((None, tm, tn), lambda b, i, j: (b, i, j)),
            p.mod_spec(l, k_gt, d, tn=tn, with_j=True),
        ],
        out_specs=pl.BlockSpec((None, tm, tn), lambda b, i, j: (b, i, j)),
        out_shape=jax.ShapeDtypeStruct((bv, lv, d), F32),
        compiler_params=_cparams(("parallel", "parallel", "arbitrary")),
        name=name,
    )(a, w, x, p.mod_operand(mod_all))


def _swiglu_kernel(a_ref, wg_ref, wu_ref, o_ref):
    a = a_ref[...]
    o_ref[...] = (_silu(_dot(a, wg_ref[...])) * _dot(a, wu_ref[...])).astype(o_ref.dtype)


def _swiglu(path, a, w_gu, l, tn, tm=None):
    bv, lv, k = a.shape
    d_ff = w_gu.shape[-1] // 2
    tm = path.tm if tm is None else tm
    nj = d_ff // tn
    assert d_ff % tn == 0
    return pl.pallas_call(
        _swiglu_kernel,
        grid=(bv, lv // tm, nj),
        in_specs=[
            pl.BlockSpec((None, tm, k), lambda b, i, j: (b, i, 0)),
            pl.BlockSpec((None, k, tn), lambda b, i, j: (l, 0, j)),
            pl.BlockSpec((None, k, tn), lambda b, i, j: (l, 0, nj + j)),
        ],
        out_specs=pl.BlockSpec((None, tm, tn), lambda b, i, j: (b, i, j)),
        out_shape=jax.ShapeDtypeStruct((bv, lv, d_ff), BF16),
        compiler_params=_cparams(("parallel", "parallel", "arbitrary")),
        name="swiglu",
    )(a, w_gu, w_gu)


def _merge_kernel(oa_ref, ob_ref, oc_ref, ma_ref, mb_ref, mc_ref, wa_ref, wb_ref, wc_ref, o_ref):
    m = jax.nn.sigmoid(ma_ref[...]) * _dot(oa_ref[...], wa_ref[...])
    m = m + jax.nn.sigmoid(mb_ref[...]) * _dot(ob_ref[...], wb_ref[...])
    m = m + jax.nn.sigmoid(mc_ref[...]) * _dot(oc_ref[...], wc_ref[...])
    o_ref[...] = m.astype(o_ref.dtype)


def _merge(path, o_a, o_b, o_c, y, gate_col0, w_br, l, tn, tm=None):
    bv, lv, wa = o_a.shape
    wb, wc = o_b.shape[-1], o_c.shape[-1]
    d = w_br.shape[-1]
    tm = path.tm if tm is None else tm
    g0 = gate_col0 // tn
    per = d // tn
    assert gate_col0 % tn == 0 and wb == wa and wc == 2 * wa
    return pl.pallas_call(
        _merge_kernel,
        grid=(bv, lv // tm, d // tn),
        in_specs=[
            pl.BlockSpec((None, tm, wa), lambda b, i, j: (b, i, 0)),
            pl.BlockSpec((None, tm, wb), lambda b, i, j: (b, i, 0)),
            pl.BlockSpec((None, tm, wc), lambda b, i, j: (b, i, 0)),
            pl.BlockSpec((None, tm, tn), lambda b, i, j: (b, i, g0 + j)),
            pl.BlockSpec((None, tm, tn), lambda b, i, j: (b, i, g0 + per + j)),
            pl.BlockSpec((None, tm, tn), lambda b, i, j: (b, i, g0 + 2 * per + j)),
            pl.BlockSpec((None, wa, tn), lambda b, i, j: (l, 0, j)),
            pl.BlockSpec((None, wb, tn), lambda b, i, j: (l, 1, j)),
            pl.BlockSpec((None, wc, tn), lambda b, i, j: (l, 1, j)),
        ],
        out_specs=pl.BlockSpec((None, tm, tn), lambda b, i, j: (b, i, j)),
        out_shape=jax.ShapeDtypeStruct((bv, lv, d), BF16),
        compiler_params=_cparams(("parallel", "parallel", "arbitrary")),
        name="merge",
    )(o_a, o_b, o_c, y, y, y, w_br, w_br, w_br)


def _lb_from_gamma(gam, l):
    depth = len(gam)
    m = gam[0]
    for i in range(1, depth):
        m = jnp.maximum(m, gam[i])
    e = [jnp.exp(gam[i] - m) for i in range(depth)]
    tot = e[0]
    for i in range(1, depth):
        tot = tot + e[i]
    sm = [ei / tot for ei in e]
    cs = [sm[0]]
    for i in range(1, depth):
        cs.append(cs[-1] + sm[i])
    return cs[l] - cs[0]


def _rec_core(qq, kk, logf, v_ref_tile, g_tile, gn, st_ref, o_ref, *, chunk, n_groups, hpg):
    c = chunk
    wq = n_groups * LANES
    dk = LANES // hpg
    n_levels = int(math.log2(c))
    row = lax.broadcasted_iota(jnp.int32, (c, wq), 0)
    ti = lax.broadcasted_iota(jnp.int32, (c, c), 0)
    si = lax.broadcasted_iota(jnp.int32, (c, c), 1)
    x = jnp.bitwise_xor(ti, si)
    lvl = jnp.zeros((c, c), jnp.int32)
    for lev in range(1, n_levels + 1):
        lvl = lvl + jnp.where(x >= (1 << (lev - 1)), 1, 0)
    lvl = jnp.where(ti > si, lvl, -1)
    lane = lax.broadcasted_iota(jnp.int32, (c, LANES), 1)

    def head_mask(a, sub):
        if hpg == 1:
            return a
        return jnp.where(lane < dk, a, 0.0) if sub == 0 else jnp.where(lane >= dk, a, 0.0)

    pre = logf
    tot = logf
    ql, kl = [], []
    for lev in range(1, n_levels + 1):
        half = 1 << (lev - 1)
        upper = (row & half) != 0
        e = jnp.exp(jnp.where(upper, pre, tot - pre))
        ql.append(qq * e)
        kl.append(kk * e)
        up = pltpu.roll(tot, half, 0)
        dn = pltpu.roll(tot, c - half, 0)
        pre = pre + jnp.where(upper, up, 0.0)
        tot = tot + jnp.where(upper, up, dn)
    q_in = qq * jnp.exp(pre)
    k_out = kk * jnp.exp(tot - pre)
    d_all = jnp.exp(tot[0:1, :])
    qk = qq * kk

    for g in range(n_groups):
        gs = slice(g * LANES, (g + 1) * LANES)
        st = st_ref[g]
        st_bf = st.astype(BF16)
        st_new = st * d_all[:, gs]
        for sub in range(hpg):
            h = g * hpg + sub
            hs = slice(h * LANES, (h + 1) * LANES)
            v_bf = v_ref_tile[:, hs].astype(BF16)
            att = jnp.zeros((c, c), F32)
            for lev in range(1, n_levels + 1):
                p = _dot_nt(head_mask(ql[lev - 1][:, gs], sub).astype(BF16), kl[lev - 1][:, gs].astype(BF16))
                att = jnp.where(lvl == lev, p, att)
            diag = jnp.sum(head_mask(qk[:, gs], sub), axis=-1, keepdims=True)
            att = jnp.where(x == 0, diag, att)
            o = _dot_nt(head_mask(q_in[:, gs], sub).astype(BF16), st_bf) + _dot(att.astype(BF16), v_bf)
            st_new = st_new + _dot_tn(v_bf, head_mask(k_out[:, gs], sub).astype(BF16))
            y = o * lax.rsqrt(jnp.mean(o * o, axis=-1, keepdims=True) + EPS) * gn
            o_ref[:, hs] = (y * _silu(g_tile[:, hs])).astype(o_ref.dtype)
        st_ref[g] = st_new


def _state_in(s0_ref, st_ref, n_groups, hpg):
    dk = LANES // hpg
    for g in range(n_groups):
        blk = jnp.concatenate([s0_ref[g * hpg + sub] for sub in range(hpg)], axis=0) if hpg > 1 else s0_ref[g]
        st_ref[g] = blk.T


def _state_out(st_ref, sout_ref, n_groups, hpg):
    dk = LANES // hpg
    for g in range(n_groups):
        t = st_ref[g].T
        for sub in range(hpg):
            sout_ref[g * hpg + sub] = t[sub * dk:(sub + 1) * dk, :]


def _rec_a_kernel(q_ref, f_ref, i_ref, g_ref, lbg_ref, gn_ref, s0_ref, o_ref, sout_ref, st_ref, *, l, chunk):
    ci = pl.program_id(1)

    @pl.when(ci == 0)
    def _():
        _state_in(s0_ref, st_ref, 4, 1)

    lb = _lb_from_gamma([lbg_ref[i:i + 1, :] for i in range(lbg_ref.shape[0])], l)
    f = f_ref[...]
    logf = _logaddexp(jnp.log(lb), jnp.log1p(-lb) + _log_sigmoid(f))
    kk = (1.0 - lb) * jax.nn.sigmoid(-f)
    qq = _silu(q_ref[...]) * (LANES ** -0.5)
    _rec_core(qq, kk, logf, i_ref, g_ref[...], gn_ref[...], st_ref, o_ref, chunk=chunk, n_groups=4, hpg=1)

    @pl.when(ci == pl.num_programs(1) - 1)
    def _():
        _state_out(st_ref, sout_ref, 4, 1)


def _rec_b_kernel(q_ref, k_ref, v_ref, g_ref, r_ref, wgk_ref, bgk_ref, gn_ref, s0_ref, o_ref, sout_ref, st_ref,
                  *, chunk):
    ci = pl.program_id(1)

    @pl.when(ci == 0)
    def _():
        _state_in(s0_ref, st_ref, 2, 2)

    gk = _dot(r_ref[...].astype(BF16), wgk_ref[...]) + bgk_ref[...]
    logf = _log_sigmoid(gk) * (1.0 / GK_NORM)
    qq = q_ref[...] * ((LANES // 2) ** -0.5)
    _rec_core(qq, k_ref[...], logf, v_ref, g_ref[...], gn_ref[...], st_ref, o_ref, chunk=chunk, n_groups=2, hpg=2)

    @pl.when(ci == pl.num_programs(1) - 1)
    def _():
        _state_out(st_ref, sout_ref, 2, 2)


def _rec_a(y, lb_gamma, gn_a, s0, l, chunk):
    b, L, _ = y.shape
    depth = lb_gamma.shape[0]
    w = 4 * LANES
    col = lambda k: pl.BlockSpec((None, chunk, w), lambda bi, ci: (bi, ci, k))
    return pl.pallas_call(
        functools.partial(_rec_a_kernel, l=l, chunk=chunk),
        grid=(b, L // chunk),
        in_specs=[
            col(0), col(1), col(2), col(3),
            pl.BlockSpec((depth, w), lambda bi, ci: (0, 0)),
            pl.BlockSpec((None, 1, LANES), lambda bi, ci: (l, 0, 0)),
            pl.BlockSpec((None, 4, LANES, LANES), lambda bi, ci: (bi, 0, 0, 0)),
        ],
        out_specs=[
            pl.BlockSpec((None, chunk, w), lambda bi, ci: (bi, ci, 0)),
            pl.BlockSpec((None, 4, LANES, LANES), lambda bi, ci: (bi, 0, 0, 0)),
        ],
        out_shape=[jax.ShapeDtypeStruct((b, L, w), BF16), jax.ShapeDtypeStruct(s0.shape, F32)],
        scratch_shapes=[pltpu.VMEM((4, LANES, LANES), F32)],
        compiler_params=_cparams(("parallel", "arbitrary")),
        name="rec_hgrn",
    )(y, y, y, y, lb_gamma, gn_a.reshape(depth, 1, LANES), s0)


def _rec_b(y, r, wgk_p, b_gk, gn_b, s0, l, chunk, col_b):
    b, L, _ = y.shape
    depth = gn_b.shape[0]
    c256 = col_b // 256
    c512 = (col_b + 512) // 512
    return pl.pallas_call(
        functools.partial(_rec_b_kernel, chunk=chunk),
        grid=(b, L // chunk),
        in_specs=[
            pl.BlockSpec((None, chunk, 256), lambda bi, ci: (bi, ci, c256)),
            pl.BlockSpec((None, chunk, 256), lambda bi, ci: (bi, ci, c256 + 1)),
            pl.BlockSpec((None, chunk, 512), lambda bi, ci: (bi, ci, c512)),
            pl.BlockSpec((None, chunk, 512), lambda bi, ci: (bi, ci, c512 + 1)),
            pl.BlockSpec((None, chunk, LANES), lambda bi, ci: (bi, ci, 0)),
            pl.BlockSpec((None, LANES, 256), lambda bi, ci: (l, 0, 0)),
            pl.BlockSpec((None, 1, 256), lambda bi, ci: (l, 0, 0)),
            pl.BlockSpec((None, 1, LANES), lambda bi, ci: (l, 0, 0)),
            pl.BlockSpec((None, 4, 64, LANES), lambda bi, ci: (bi, 0, 0, 0)),
        ],
        out_specs=[
            pl.BlockSpec((None, chunk, 512), lambda bi, ci: (bi, ci, 0)),
            pl.BlockSpec((None, 4, 64, LANES), lambda bi, ci: (bi, 0, 0, 0)),
        ],
        out_shape=[jax.ShapeDtypeStruct((b, L, 512), BF16), jax.ShapeDtypeStruct(s0.shape, F32)],
        scratch_shapes=[pltpu.VMEM((2, LANES, LANES), F32)],
        compiler_params=_cparams(("parallel", "arbitrary")),
        name="rec_gla",
    )(y, y, y, y, r, wgk_p, b_gk.reshape(depth, 1, 256), gn_b.reshape(depth, 1, LANES), s0)


def _dec_finish(s_new, q_col, g_row, gn, o_ref, h):
    o = jnp.sum(q_col * s_new, axis=0, keepdims=True)
    y = o * lax.rsqrt(jnp.mean(o * o, axis=-1, keepdims=True) + EPS) * gn
    o_ref[h] = y * _silu(g_row)


def _dec_a_kernel(q_ref, f_ref, i_ref, g_ref, lbg_ref, gn_ref, s0_ref, o_ref, sout_ref, *, l):
    lb_all = _lb_from_gamma([lbg_ref[i] for i in range(lbg_ref.shape[0])], l)
    for h in range(4):
        lb = lb_all[h]
        f = f_ref[h]
        logf = _logaddexp(jnp.log(lb), jnp.log1p(-lb) + _log_sigmoid(f))
        kk = (1.0 - lb) * jax.nn.sigmoid(-f)
        qq = _silu(q_ref[h]) * (LANES ** -0.5)
        s_new = jnp.exp(logf) * s0_ref[h] + kk * i_ref[h]
        sout_ref[h] = s_new
        _dec_finish(s_new, qq, g_ref[h], gn_ref[...], o_ref, h)


def _dec_b_kernel(q_ref, k_ref, v_ref, g_ref, r_ref, wgk_ref, bgk_ref, gn_ref, s0_ref, o_ref, sout_ref):
    r = r_ref[...]
    for h in range(4):
        gk = jnp.sum(wgk_ref[h] * r, axis=-1, keepdims=True) + bgk_ref[h]
        logf = _log_sigmoid(gk) * (1.0 / GK_NORM)
        qq = q_ref[h] * ((LANES // 2) ** -0.5)
        s_new = jnp.exp(logf) * s0_ref[h] + k_ref[h] * v_ref[h]
        sout_ref[h] = s_new
        _dec_finish(s_new, qq, g_ref[h], gn_ref[...], o_ref, h)


def _dec_a(qa, fa, ia, ga, lb_gamma, gn_a, s0, l):
    nb = qa.shape[0]
    depth = lb_gamma.shape[0]
    colspec = pl.BlockSpec((None, 4, LANES, 1), lambda b: (b, 0, 0, 0))
    rowspec = pl.BlockSpec((None, 4, 1, LANES), lambda b: (b, 0, 0, 0))
    stspec = pl.BlockSpec((None, 4, LANES, LANES), lambda b: (b, 0, 0, 0))
    o, s = pl.pallas_call(
        functools.partial(_dec_a_kernel, l=l),
        grid=(nb,),
        in_specs=[colspec, colspec, rowspec, rowspec,
                  pl.BlockSpec((depth, 4, LANES, 1), lambda b: (0, 0, 0, 0)),
                  pl.BlockSpec((None, 1, LANES), lambda b: (l, 0, 0)),
                  stspec],
        out_specs=[rowspec, stspec],
        out_shape=[jax.ShapeDtypeStruct((nb, 4, 1, LANES), F32), jax.ShapeDtypeStruct(s0.shape, F32)],
        compiler_params=_cparams(("parallel",)),
        name="dec_hgrn",
    )(qa.reshape(nb, 4, LANES, 1), fa.reshape(nb, 4, LANES, 1), ia.reshape(nb, 4, 1, LANES),
      ga.reshape(nb, 4, 1, LANES), lb_gamma.reshape(depth, 4, LANES, 1), gn_a.reshape(depth, 1, LANES), s0)
    return o.reshape(nb, 1, 4 * LANES).astype(BF16), s


def _dec_b(qb, kb, vb, gb, rb, wgk_t, b_gk, gn_b, s0, l):
    nb = qb.shape[0]
    depth = gn_b.shape[0]
    rank = rb.shape[-1]
    colspec = pl.BlockSpec((None, 4, 64, 1), lambda b: (b, 0, 0, 0))
    rowspec = pl.BlockSpec((None, 4, 1, LANES), lambda b: (b, 0, 0, 0))
    stspec = pl.BlockSpec((None, 4, 64, LANES), lambda b: (b, 0, 0, 0))
    o, s = pl.pallas_call(
        _dec_b_kernel,
        grid=(nb,),
        in_specs=[colspec, colspec, rowspec, rowspec,
                  pl.BlockSpec((None, 1, rank), lambda b: (b, 0, 0)),
                  pl.BlockSpec((None, 4, 64, rank), lambda b: (l, 0, 0, 0)),
                  pl.BlockSpec((None, 4, 64, 1), lambda b: (l, 0, 0, 0)),
                  pl.BlockSpec((None, 1, LANES), lambda b: (l, 0, 0)),
                  stspec],
        out_specs=[rowspec, stspec],
        out_shape=[jax.ShapeDtypeStruct((nb, 4, 1, LANES), F32), jax.ShapeDtypeStruct(s0.shape, F32)],
        compiler_params=_cparams(("parallel",)),
        name="dec_gla",
    )(qb.reshape(nb, 4, 64, 1), kb.reshape(nb, 4, 64, 1), vb.reshape(nb, 4, 1, LANES),
      gb.reshape(nb, 4, 1, LANES), rb.reshape(nb, 1, rank), wgk_t, b_gk.reshape(depth, 4, 64, 1),
      gn_b.reshape(depth, 1, LANES), s0)
    return o.reshape(nb, 1, 4 * LANES).astype(BF16), s


def _suffix_matrix(n, rows_first):
    a = lax.broadcasted_iota(jnp.int32, (n, n), 0)
    b = lax.broadcasted_iota(jnp.int32, (n, n), 1)
    if rows_first:
        m = jnp.where(a >= b, 1.0, 0.0).astype(BF16)
        return jnp.concatenate([m, m], axis=0)
    m = jnp.where(b >= a, 1.0, 0.0).astype(BF16)
    return jnp.concatenate([m, m], axis=1)


def _sb_prompt_kernel(bias_ref, q_ref, k_ref, v_ref, o_ref, kbf_ref, vbf_ref, z_ref, sp_ref, suf_ref,
                      *, l, tq, scale, unroll):
    h = pl.program_id(1)
    qi = pl.program_id(2)

    @pl.when(qi == 0)
    def _():
        kbf_ref[...] = k_ref[...].astype(BF16)
        vbf_ref[...] = v_ref[...].astype(BF16)

    bias = bias_ref[l, h]
    q = q_ref[...].astype(BF16)
    uu = _suffix_matrix(tq, rows_first=True)
    ti = lax.broadcasted_iota(jnp.int32, (tq, tq), 0)
    si = lax.broadcasted_iota(jnp.int32, (tq, tq), 1)
    causal = si < ti

    def logits(kb, masked):
        start = pl.multiple_of(kb * tq, tq)
        z = _dot_nt(q, kbf_ref[pl.ds(start, tq), :]) * scale + bias
        sp = jnp.maximum(z, 0.0) + jnp.log(1.0 + jnp.exp(-jnp.abs(z)))
        if masked:
            sp = jnp.where(causal, sp, 0.0)
        hi = sp.astype(BF16)
        z_ref[kb] = z
        sp_ref[kb, :, 0:tq] = hi
        sp_ref[kb, :, tq:2 * tq] = (sp - hi.astype(F32)).astype(BF16)

    def suffix(kb):
        suf_ref[kb] = _dot(sp_ref[kb], uu)

    def weigh(kb, run, acc, masked):
        start = pl.multiple_of(kb * tq, tq)
        suf = suf_ref[kb]
        w = jnp.exp(z_ref[kb] - suf - run)
        if masked:
            w = jnp.where(causal, w, 0.0)
        acc = acc + _dot(w.astype(BF16), vbf_ref[pl.ds(start, tq), :])
        return run + suf[:, 0:1], acc

    n_main = qi // unroll
    n_tail = qi - unroll * n_main

    def sweep(fn):
        def main(j, c):
            for u in range(unroll):
                fn(unroll * j + u)
            return c
        lax.fori_loop(0, n_main, main, 0)
        lax.fori_loop(0, n_tail, lambda j, c: (fn(unroll * n_main + j), c)[1], 0)

    logits(qi, True)
    sweep(lambda kb: logits(kb, False))
    suffix(qi)
    sweep(suffix)

    carry = weigh(qi, jnp.zeros((tq, 1), F32), jnp.zeros((tq, LANES), F32), True)

    def main3(j, c):
        for u in range(unroll):
            c = weigh(qi - 1 - (unroll * j + u), c[0], c[1], False)
        return c

    carry = lax.fori_loop(0, n_main, main3, carry)
    carry = lax.fori_loop(0, n_tail, lambda j, c: weigh(n_tail - 1 - j, c[0], c[1], False), carry)
    o_ref[...] = carry[1].astype(o_ref.dtype)


def _sb_prompt(y, b_sb, l, col_q, n_heads, tq):
    b, L, _ = y.shape
    cq = col_q // LANES
    scale = LANES ** -0.5
    return pl.pallas_call(
        functools.partial(_sb_prompt_kernel, l=l, tq=tq, scale=scale, unroll=SB_UNROLL),
        grid=(b, n_heads, L // tq),
        in_specs=[
            pl.BlockSpec(memory_space=pltpu.SMEM),
            pl.BlockSpec((None, tq, LANES), lambda bi, h, qi: (bi, qi, cq + h)),
            pl.BlockSpec((None, L, LANES), lambda bi, h, qi: (bi, 0, cq + n_heads + h)),
            pl.BlockSpec((None, L, LANES), lambda bi, h, qi: (bi, 0, cq + 2 * n_heads + h)),
        ],
        out_specs=pl.BlockSpec((None, tq, LANES), lambda bi, h, qi: (bi, qi, h)),
        out_shape=jax.ShapeDtypeStruct((b, L, n_heads * LANES), BF16),
        scratch_shapes=[pltpu.VMEM((L, LANES), BF16), pltpu.VMEM((L, LANES), BF16),
                        pltpu.VMEM((L // tq, tq, tq), F32), pltpu.VMEM((L // tq, tq, 2 * tq), BF16),
                        pltpu.VMEM((L // tq, tq, tq), F32)],
        compiler_params=_cparams(("parallel", "parallel", "arbitrary")),
        name="sb_prompt",
    )(b_sb, y, y, y)


def _sb_decode_kernel(pt_ref, bias_ref, q_ref, *refs, page, npg, scale):
    k_refs, v_refs = refs[:npg], refs[npg:2 * npg]
    o_ref, run_ref, acc_ref = refs[2 * npg:]
    p = pl.program_id(1)

    @pl.when(p == 0)
    def _():
        run_ref[...] = jnp.zeros_like(run_ref)
        acc_ref[...] = jnp.zeros_like(acc_ref)

    q = q_ref[...]
    bias = bias_ref[...]
    nh = q.shape[0]
    tok = lax.broadcasted_iota(jnp.int32, (page, nh, LANES), 0)
    lane = lax.broadcasted_iota(jnp.int32, (page, nh, LANES), 2)
    own_lane = tok == lane
    ones = jnp.ones((LANES, LANES), BF16)
    uu = _suffix_matrix(page, rows_first=True)
    run = run_ref[...]
    acc = acc_ref[...]
    for i in range(npg):
        prod = (k_refs[i][...] * q[None]).reshape(page * nh, LANES).astype(BF16)
        zrep = _dot(prod, ones).reshape(page, nh, LANES)
        z = jnp.sum(jnp.where(own_lane, zrep, 0.0), axis=0) * scale + bias
        lk = -(jnp.maximum(z, 0.0) + jnp.log(1.0 + jnp.exp(-jnp.abs(z))))
        hi = lk.astype(BF16)
        lo = (lk - hi.astype(F32)).astype(BF16)
        suf = _dot(jnp.concatenate([hi, lo], axis=1), uu)
        w = jnp.exp(z + suf + run)
        wsel = jnp.where(own_lane, w[None], 0.0).reshape(page * nh, LANES).astype(BF16)
        wrep = _dot(wsel, ones).reshape(page, nh, LANES)
        acc = acc + jnp.sum(wrep * v_refs[i][...], axis=0)
        run = run + suf[:, 0:1]
    run_ref[...] = run
    acc_ref[...] = acc

    @pl.when(p == pl.num_programs(1) - 1)
    def _():
        o_ref[...] = acc


def _sb_decode(q, cache_k, cache_v, page_table, b_sb, l):
    nb = q.shape[0]
    _, _, page, nh, dh = cache_k.shape
    n_pages = page_table.shape[1]
    npg = math.gcd(DEC_PAGES, n_pages)
    assert nh == SUBLANES and dh == LANES and page == LANES
    bias = jnp.broadcast_to(b_sb[l][:, None], (nh, LANES))

    def page_spec(i):
        return pl.BlockSpec((None, None, page, nh, dh),
                            lambda b, p, pt: (l, pt[b, n_pages - 1 - (p * npg + i)], 0, 0, 0))

    grid_spec = pltpu.PrefetchScalarGridSpec(
        num_scalar_prefetch=1,
        grid=(nb, n_pages // npg),
        in_specs=[pl.BlockSpec((nh, LANES), lambda b, p, pt: (0, 0)),
                  pl.BlockSpec((None, nh, dh), lambda b, p, pt: (b, 0, 0))]
                 + [page_spec(i) for i in range(npg)] * 2,
        out_specs=pl.BlockSpec((None, nh, dh), lambda b, p, pt: (b, 0, 0)),
        scratch_shapes=[pltpu.VMEM((nh, LANES), F32), pltpu.VMEM((nh, dh), F32)],
    )
    o = pl.pallas_call(
        functools.partial(_sb_decode_kernel, page=page, npg=npg, scale=dh ** -0.5),
        grid_spec=grid_spec,
        out_shape=jax.ShapeDtypeStruct((nb, nh, dh), F32),
        compiler_params=_cparams(("parallel", "arbitrary")),
        name="sb_decode",
    )(page_table, bias, q.reshape(nb, nh, dh), *([cache_k] * npg), *([cache_v] * npg))
    return o.reshape(nb, 1, nh * dh).astype(BF16)


def _trunk(path, x, mod_all, wts, state_a, state_b, cache, dims):
    d_model = x.shape[-1]
    depth = wts["w_in"].shape[0]
    col = dims["col"]
    is_prompt = cache is None
    tm = path.tm
    ks, vs, sas, sbs = [], [], [], []
    for l in range(depth):
        h = _norm_mod(path, x, wts["g_mix"], mod_all, l, 1, 0)
        y = _mm(path, h, wts["w_in"], l, 0, col["rb"], 512, name="in_proj")
        r = _mm(path, h, wts["w_in"], l, col["rb"], LANES, LANES, name="in_proj_r")
        bv, lv, _ = y.shape
        k_new = y[:, :, col["kc"]:col["vc"]]
        v_new = y[:, :, col["vc"]:col["ma"]]
        if is_prompt:
            o_a, s_a = _rec_a(y, wts["lb_gamma"], wts["gn_a"], state_a[l], l, dims["chunk"])
            o_b, s_b = _rec_b(y, r, wts["w_gk_p"], wts["b_gk"], wts["gn_b"], state_b[l], l, dims["chunk"], col["qb"])
            o_c = _sb_prompt(y, wts["b_sb"], l, col["qc"], dims["h_c"], dims["tq"])
        else:
            y2 = y.reshape(lv, -1)
            sl = lambda a, b_: y2[:, col[a]:col[b_]]
            o_a, s_a = _dec_a(sl("qa", "fa"), sl("fa", "ia"), sl("ia", "ga"), sl("ga", "qb"),
                              wts["lb_gamma"], wts["gn_a"], state_a[l], l)
            o_b, s_b = _dec_b(sl("qb", "kb"), sl("kb", "vb"), sl("vb", "gb"), sl("gb", "qc"),
                              r.reshape(lv, -1)[:, :dims["gk_rank"]], wts["w_gk_t"], wts["b_gk"], wts["gn_b"],
                              state_b[l], l)
            o_c = _sb_decode(sl("qc", "kc"), cache[0], cache[1], cache[2], wts["b_sb"], l)
            o_a, o_b, o_c = (o.reshape(bv, lv, -1) for o in (o_a, o_b, o_c))
        merged = _merge(path, o_a, o_b, o_c, y, col["ma"], wts["w_br"], l, 512)
        x = _mm_resid(path, merged, wts["w_o"], x, mod_all, l, 2, 512, name="out_proj")
        h2 = _norm_mod(path, x, wts["g_ffn"], mod_all, l, 4, 3)
        act = _swiglu(path, h2, wts["w_gu"], l, 512)
        x = _mm_resid(path, act, wts["w_down"], x, mod_all, l, 5, 512, tm=min(tm, 512), name="down_proj")
        ks.append(k_new)
        vs.append(v_new)
        sas.append(s_a)
        sbs.append(s_b)
    xf = _final_norm(x, wts["g_final"], min(tm, 512))
    return xf, jnp.stack(ks), jnp.stack(vs), jnp.stack(sas), jnp.stack(sbs)


def kernel(x_prompt, x_sample, cache_k, cache_v, state_hgrn, state_gla, page_table, c_prompt, c_sample, w_ada, b_ada, g_mix, w_in, b_sb, lb_gamma, gn_a, w_gk, b_gk, gn_b, w_br, w_o, g_ffn, w_gu, w_down, g_final):
    bp, seq, d_model = x_prompt.shape
    nb = x_sample.shape[0]
    depth, _, _, h_c, dh_c = cache_k.shape[0], None, None, cache_k.shape[3], cache_k.shape[4]
    h_a, dk_a, dv_a = state_hgrn.shape[2:]
    h_b, dk_b, dv_b = state_gla.shape[2:]
    gk_rank = w_gk.shape[1]
    w_a, w_b, w_c = h_a * dv_a, h_b * dv_b, h_c * dh_c

    widths = [("qa", h_a * dk_a), ("fa", h_a * dk_a), ("ia", w_a), ("ga", w_a),
              ("qb", h_b * dk_b), ("kb", h_b * dk_b), ("vb", w_b), ("gb", w_b),
              ("qc", w_c), ("kc", w_c), ("vc", w_c),
              ("ma", d_model), ("mb", d_model), ("mc", d_model)]
    col, off = {}, 0
    for name, wd in widths:
        col[name] = off
        off += wd
    col["rb"] = off
    src_rb = 2 * h_a * dk_a + 2 * w_a + 2 * h_b * dk_b + 2 * w_b
    w_in_r = jnp.concatenate(
        [w_in[:, :, :src_rb], w_in[:, :, src_rb + gk_rank:], w_in[:, :, src_rb:src_rb + gk_rank],
         jnp.zeros((depth, d_model, LANES - gk_rank), w_in.dtype)], axis=-1).astype(BF16)
    w_gk_p = jnp.concatenate([w_gk, jnp.zeros((depth, LANES - gk_rank, w_gk.shape[-1]), w_gk.dtype)],
                             axis=1).astype(BF16)
    wts = {
        "w_in": w_in_r, "w_br": w_br.astype(BF16), "w_o": w_o.astype(BF16), "w_gu": w_gu.astype(BF16),
        "w_down": w_down.astype(BF16), "w_gk_p": w_gk_p,
        "w_gk_t": jnp.swapaxes(w_gk, 1, 2).reshape(depth, h_b, dk_b, gk_rank),
        "g_mix": g_mix, "g_ffn": g_ffn, "g_final": g_final, "b_sb": b_sb, "lb_gamma": lb_gamma,
        "gn_a": gn_a, "gn_b": gn_b, "b_gk": b_gk,
    }
    dims = {"col": col, "chunk": 256, "tq": 256, "h_c": h_c, "gk_rank": gk_rank}

    c_all = jnp.concatenate([c_sample, c_prompt, jnp.zeros((16 - nb - bp, d_model), F32)], axis=0)
    mod_all = _ada_mod(c_all, w_ada, b_ada)

    prompt = _Path(bp, seq, min(1024, seq), per_row_mod=False, mod_row0=nb)
    sample = _Path(1, nb, nb, per_row_mod=True, mod_row0=0)

    zeros_a = jnp.zeros((depth, bp) + state_hgrn.shape[2:], state_hgrn.dtype)
    zeros_b = jnp.zeros((depth, bp) + state_gla.shape[2:], state_gla.dtype)
    y_p, k_p, v_p, sa_p, sb_p = _trunk(prompt, x_prompt, mod_all, wts, zeros_a, zeros_b, None, dims)

    cache = (cache_k, cache_v, page_table)
    y_s, k_s, v_s, sa_s, sb_s = _trunk(sample, x_sample.reshape(1, nb, d_model), mod_all, wts,
                                       state_hgrn, state_gla, cache, dims)

    return (y_p, y_s.reshape(nb, 1, d_model),
            k_p.reshape(depth, bp, seq, h_c, dh_c), v_p.reshape(depth, bp, seq, h_c, dh_c),
            k_s.reshape(depth, nb, 1, h_c, dh_c), v_s.reshape(depth, nb, 1, h_c, dh_c),
            sa_p, sa_s, sb_p, sb_s)
```

```python
import functools
import math

import jax
import jax.numpy as jnp
from jax import lax
from jax.experimental import pallas as pl
from jax.experimental.pallas import tpu as pltpu

F32 = jnp.float32
BF16 = jnp.bfloat16

EPS = 1e-6
GK_NORM = 16.0
N_MOD = 6
LANES = 128
SUBLANES = 8
VMEM_LIMIT = 56 * 1024 * 1024
DEC_PAGES = 8


def _cparams(sem):
    return pltpu.CompilerParams(dimension_semantics=sem, vmem_limit_bytes=VMEM_LIMIT)


def _silu(x):
    return x * jax.nn.sigmoid(x)


def _log_sigmoid(x):
    return jnp.minimum(x, 0.0) - jnp.log1p(jnp.exp(-jnp.abs(x)))


def _logaddexp(a, b):
    m = jnp.maximum(a, b)
    return m + jnp.log1p(jnp.exp(-jnp.abs(a - b)))


def _dot(a, b):
    return jnp.dot(a, b, preferred_element_type=F32)


def _dot_nt(a, b):
    return lax.dot_general(a, b, (((1,), (1,)), ((), ())), preferred_element_type=F32)


def _dot_tn(a, b):
    return lax.dot_general(a, b, (((0,), (0,)), ((), ())), preferred_element_type=F32)


class _Path:
    def __init__(self, bv, lv, tm, per_row_mod, mod_row0):
        self.bv, self.lv, self.tm = bv, lv, tm
        self.per_row_mod = per_row_mod
        self.mod_row0 = mod_row0

    def grid_rows(self):
        return (self.bv, self.lv // self.tm)

    def mod_operand(self, mod_all):
        if self.per_row_mod:
            return mod_all
        d, r, w = mod_all.shape
        return mod_all.reshape(d, r, 1, w)

    def mod_spec(self, l, k, d_model, tn=None, with_j=False):
        tn = d_model if tn is None else tn
        per = d_model // tn
        if self.per_row_mod:
            if with_j:
                return pl.BlockSpec((None, self.tm, tn), lambda b, i, j: (l, i, k * per + j))
            return pl.BlockSpec((None, self.tm, tn), lambda b, i: (l, i, k * per))
        r0 = self.mod_row0
        if with_j:
            return pl.BlockSpec((None, None, 1, tn), lambda b, i, j: (l, r0 + b, 0, k * per + j))
        return pl.BlockSpec((None, None, 1, tn), lambda b, i: (l, r0 + b, 0, k * per))


def _ada_kernel(c_ref, w_ref, b_ref, o_ref):
    a = _silu(c_ref[...]).astype(BF16)
    o_ref[...] = _dot(a, w_ref[...].astype(BF16)) + b_ref[...]


def _ada_mod(c_all, w_ada, b_ada, tn=1024):
    depth, d, n = w_ada.shape
    rows = c_all.shape[0]
    return pl.pallas_call(
        _ada_kernel,
        grid=(depth, n // tn),
        in_specs=[
            pl.BlockSpec((rows, d), lambda l, j: (0, 0)),
            pl.BlockSpec((None, d, tn), lambda l, j: (l, 0, j)),
            pl.BlockSpec((None, 1, tn), lambda l, j: (l, 0, j)),
        ],
        out_specs=pl.BlockSpec((None, rows, tn), lambda l, j: (l, 0, j)),
        out_shape=jax.ShapeDtypeStruct((depth, rows, n), F32),
        compiler_params=_cparams(("parallel", "parallel")),
        name="ada_mod",
    )(c_all, w_ada, b_ada.reshape(depth, 1, n))


def _norm_mod_kernel(x_ref, g_ref, sc_ref, sh_ref, o_ref):
    x = x_ref[...]
    y = x * lax.rsqrt(jnp.mean(x * x, axis=-1, keepdims=True) + EPS) * g_ref[...]
    o_ref[...] = (y * (1.0 + sc_ref[...]) + sh_ref[...]).astype(o_ref.dtype)


def _norm_kernel(x_ref, g_ref, o_ref):
    x = x_ref[...]
    y = x * lax.rsqrt(jnp.mean(x * x, axis=-1, keepdims=True) + EPS) * g_ref[...]
    o_ref[...] = y.astype(o_ref.dtype)


def _norm_mod(path, x, g, mod_all, l, k_sc, k_sh, tm=None):
    bv, lv, d = x.shape
    tm = min(path.tm, 512) if tm is None else tm
    p = _Path(bv, lv, tm, path.per_row_mod, path.mod_row0)
    depth = g.shape[0]
    mod = p.mod_operand(mod_all)
    return pl.pallas_call(
        _norm_mod_kernel,
        grid=p.grid_rows(),
        in_specs=[
            pl.BlockSpec((None, tm, d), lambda b, i: (b, i, 0)),
            pl.BlockSpec((None, 1, d), lambda b, i: (l, 0, 0)),
            p.mod_spec(l, k_sc, d),
            p.mod_spec(l, k_sh, d),
        ],
        out_specs=pl.BlockSpec((None, tm, d), lambda b, i: (b, i, 0)),
        out_shape=jax.ShapeDtypeStruct((bv, lv, d), BF16),
        compiler_params=_cparams(("parallel", "parallel")),
        name="norm_mod",
    )(x, g.reshape(depth, 1, d), mod, mod)


def _final_norm(x, g, tm):
    bv, lv, d = x.shape
    return pl.pallas_call(
        _norm_kernel,
        grid=(bv, lv // tm),
        in_specs=[
            pl.BlockSpec((None, tm, d), lambda b, i: (b, i, 0)),
            pl.BlockSpec((1, d), lambda b, i: (0, 0)),
        ],
        out_specs=pl.BlockSpec((None, tm, d), lambda b, i: (b, i, 0)),
        out_shape=jax.ShapeDtypeStruct((bv, lv, d), F32),
        compiler_params=_cparams(("parallel", "parallel")),
        name="final_norm",
    )(x, g.reshape(1, d))


def _mm_kernel(a_ref, w_ref, o_ref):
    o_ref[...] = _dot(a_ref[...], w_ref[...]).astype(o_ref.dtype)


def _mm(path, a, w, l, col0, ncols, tn, tm=None, out_dtype=F32, name="mm"):
    bv, lv, k = a.shape
    tm = path.tm if tm is None else tm
    c0 = col0 // tn
    assert col0 % tn == 0 and ncols % tn == 0 and lv % tm == 0
    return pl.pallas_call(
        _mm_kernel,
        grid=(bv, lv // tm, ncols // tn),
        in_specs=[
            pl.BlockSpec((None, tm, k), lambda b, i, j: (b, i, 0)),
            pl.BlockSpec((None, k, tn), lambda b, i, j: (l, 0, c0 + j)),
        ],
        out_specs=pl.BlockSpec((None, tm, tn), lambda b, i, j: (b, i, j)),
        out_shape=jax.ShapeDtypeStruct((bv, lv, ncols), out_dtype),
        compiler_params=_cparams(("parallel", "parallel", "arbitrary")),
        name=name,
    )(a, w)


def _mm_resid_kernel(a_ref, w_ref, x_ref, gt_ref, o_ref):
    o_ref[...] = x_ref[...] + gt_ref[...] * _dot(a_ref[...], w_ref[...])


def _mm_resid(path, a, w, x, mod_all, l, k_gt, tn, tm=None, name="mm_resid"):
    bv, lv, k = a.shape
    d = w.shape[-1]
    tm = path.tm if tm is None else tm
    p = _Path(bv, lv, tm, path.per_row_mod, path.mod_row0)
    return pl.pallas_call(
        _mm_resid_kernel,
        grid=(bv, lv // tm, d // tn),
        in_specs=[
            pl.BlockSpec((None, tm, k), lambda b, i, j: (b, i, 0)),
            pl.BlockSpec((None, k, tn), lambda b, i, j: (l, 0, j)),
            pl.BlockSpec((None, tm, tn), lambda b, i, j: (b, i, j)),
            p.mod_spec(l, k_gt, d, tn=tn, with_j=True),
        ],
        out_specs=pl.BlockSpec((None, tm, tn), lambda b, i, j: (b, i, j)),
        out_shape=jax.ShapeDtypeStruct((bv, lv, d), F32),
        compiler_params=_cparams(("parallel", "parallel", "arbitrary")),
        name=name,
    )(a, w, x, p.mod_operand(mod_all))


def _swiglu_kernel(a_ref, wg_ref, wu_ref, o_ref):
    a = a_ref[...]
    o_ref[...] = (_silu(_dot(a, wg_ref[...])) * _dot(a, wu_ref[...])).astype(o_ref.dtype)


def _swiglu(path, a, w_gu, l, tn, tm=None):
    bv, lv, k = a.shape
    d_ff = w_gu.shape[-1] // 2
    tm = path.tm if tm is None else tm
    nj = d_ff // tn
    assert d_ff % tn == 0
    return pl.pallas_call(
        _swiglu_kernel,
        grid=(bv, lv // tm, nj),
        in_specs=[
            pl.BlockSpec((None, tm, k), lambda b, i, j: (b, i, 0)),
            pl.BlockSpec((None, k, tn), lambda b, i, j: (l, 0, j)),
            pl.BlockSpec((None, k, tn), lambda b, i, j: (l, 0, nj + j)),
        ],
        out_specs=pl.BlockSpec((None, tm, tn), lambda b, i, j: (b, i, j)),
        out_shape=jax.ShapeDtypeStruct((bv, lv, d_ff), BF16),
        compiler_params=_cparams(("parallel", "parallel", "arbitrary")),
        name="swiglu",
    )(a, w_gu, w_gu)


def _merge_kernel(oa_ref, ob_ref, oc_ref, ma_ref, mb_ref, mc_ref, wa_ref, wb_ref, wc_ref, o_ref):
    m = jax.nn.sigmoid(ma_ref[...]) * _dot(oa_ref[...], wa_ref[...])
    m = m + jax.nn.sigmoid(mb_ref[...]) * _dot(ob_ref[...], wb_ref[...])
    m = m + jax.nn.sigmoid(mc_ref[...]) * _dot(oc_ref[...], wc_ref[...])
    o_ref[...] = m.astype(o_ref.dtype)


def _merge(path, o_a, o_b, o_c, y, gate_col0, w_br, l, tn, tm=None):
    bv, lv, wa = o_a.shape
    wb, wc = o_b.shape[-1], o_c.shape[-1]
    d = w_br.shape[-1]
    tm = path.tm if tm is None else tm
    g0 = gate_col0 // tn
    per = d // tn
    assert gate_col0 % tn == 0 and wb == wa and wc == 2 * wa
    return pl.pallas_call(
        _merge_kernel,
        grid=(bv, lv // tm, d // tn),
        in_specs=[
            pl.BlockSpec((None, tm, wa), lambda b, i, j: (b, i, 0)),
            pl.BlockSpec((None, tm, wb), lambda b, i, j: (b, i, 0)),
            pl.BlockSpec((None, tm, wc), lambda b, i, j: (b, i, 0)),
            pl.BlockSpec((None, tm, tn), lambda b, i, j: (b, i, g0 + j)),
            pl.BlockSpec((None, tm, tn), lambda b, i, j: (b, i, g0 + per + j)),
            pl.BlockSpec((None, tm, tn), lambda b, i, j: (b, i, g0 + 2 * per + j)),
            pl.BlockSpec((None, wa, tn), lambda b, i, j: (l, 0, j)),
            pl.BlockSpec((None, wb, tn), lambda b, i, j: (l, 1, j)),
            pl.BlockSpec((None, wc, tn), lambda b, i, j: (l, 1, j)),
        ],
        out_specs=pl.BlockSpec((None, tm, tn), lambda b, i, j: (b, i, j)),
        out_shape=jax.ShapeDtypeStruct((bv, lv, d), BF16),
        compiler_params=_cparams(("parallel", "parallel", "arbitrary")),
        name="merge",
    )(o_a, o_b, o_c, y, y, y, w_br, w_br, w_br)


def _lb_from_gamma(gam, l):
    depth = len(gam)
    m = gam[0]
    for i in range(1, depth):
        m = jnp.maximum(m, gam[i])
    e = [jnp.exp(gam[i] - m) for i in range(depth)]
    tot = e[0]
    for i in range(1, depth):
        tot = tot + e[i]
    sm = [ei / tot for ei in e]
    cs = [sm[0]]
    for i in range(1, depth):
        cs.append(cs[-1] + sm[i])
    return cs[l] - cs[0]


def _rec_core(qq, kk, logf, v_ref_tile, g_tile, gn, st_ref, o_ref, *, chunk, n_groups, hpg):
    c = chunk
    wq = n_groups * LANES
    dk = LANES // hpg
    n_levels = int(math.log2(c))
    row = lax.broadcasted_iota(jnp.int32, (c, wq), 0)
    ti = lax.broadcasted_iota(jnp.int32, (c, c), 0)
    si = lax.broadcasted_iota(jnp.int32, (c, c), 1)
    x = jnp.bitwise_xor(ti, si)
    lvl = jnp.zeros((c, c), jnp.int32)
    for lev in range(1, n_levels + 1):
        lvl = lvl + jnp.where(x >= (1 << (lev - 1)), 1, 0)
    lvl = jnp.where(ti > si, lvl, -1)
    lane = lax.broadcasted_iota(jnp.int32, (c, LANES), 1)

    def head_mask(a, sub):
        if hpg == 1:
            return a
        return jnp.where(lane < dk, a, 0.0) if sub == 0 else jnp.where(lane >= dk, a, 0.0)

    pre = logf
    tot = logf
    ql, kl = [], []
    for lev in range(1, n_levels + 1):
        half = 1 << (lev - 1)
        upper = (row & half) != 0
        e = jnp.exp(jnp.where(upper, pre, tot - pre))
        ql.append(qq * e)
        kl.append(kk * e)
        up = pltpu.roll(tot, half, 0)
        dn = pltpu.roll(tot, c - half, 0)
        pre = pre + jnp.where(upper, up, 0.0)
        tot = tot + jnp.where(upper, up, dn)
    q_in = qq * jnp.exp(pre)
    k_out = kk * jnp.exp(tot - pre)
    d_all = jnp.exp(tot[0:1, :])
    qk = qq * kk

    for g in range(n_groups):
        gs = slice(g * LANES, (g + 1) * LANES)
        st = st_ref[g]
        st_bf = st.astype(BF16)
        st_new = st * d_all[:, gs]
        for sub in range(hpg):
            h = g * hpg + sub
            hs = slice(h * LANES, (h + 1) * LANES)
            v_bf = v_ref_tile[:, hs].astype(BF16)
            att = jnp.zeros((c, c), F32)
            for lev in range(1, n_levels + 1):
                p = _dot_nt(head_mask(ql[lev - 1][:, gs], sub).astype(BF16), kl[lev - 1][:, gs].astype(BF16))
                att = jnp.where(lvl == lev, p, att)
            diag = jnp.sum(head_mask(qk[:, gs], sub), axis=-1, keepdims=True)
            att = jnp.where(x == 0, diag, att)
            o = _dot_nt(head_mask(q_in[:, gs], sub).astype(BF16), st_bf) + _dot(att.astype(BF16), v_bf)
            st_new = st_new + _dot_tn(v_bf, head_mask(k_out[:, gs], sub).astype(BF16))
            y = o * lax.rsqrt(jnp.mean(o * o, axis=-1, keepdims=True) + EPS) * gn
            o_ref[:, hs] = (y * _silu(g_tile[:, hs])).astype(o_ref.dtype)
        st_ref[g] = st_new


def _state_in(s0_ref, st_ref, n_groups, hpg):
    dk = LANES // hpg
    for g in range(n_groups):
        blk = jnp.concatenate([s0_ref[g * hpg + sub] for sub in range(hpg)], axis=0) if hpg > 1 else s0_ref[g]
        st_ref[g] = blk.T


def _state_out(st_ref, sout_ref, n_groups, hpg):
    dk = LANES // hpg
    for g in range(n_groups):
        t = st_ref[g].T
        for sub in range(hpg):
            sout_ref[g * hpg + sub] = t[sub * dk:(sub + 1) * dk, :]


def _rec_a_kernel(q_ref, f_ref, i_ref, g_ref, lbg_ref, gn_ref, s0_ref, o_ref, sout_ref, st_ref, *, l, chunk):
    ci = pl.program_id(1)

    @pl.when(ci == 0)
    def _():
        _state_in(s0_ref, st_ref, 4, 1)

    lb = _lb_from_gamma([lbg_ref[i:i + 1, :] for i in range(lbg_ref.shape[0])], l)
    f = f_ref[...]
    logf = _logaddexp(jnp.log(lb), jnp.log1p(-lb) + _log_sigmoid(f))
    kk = (1.0 - lb) * jax.nn.sigmoid(-f)
    qq = _silu(q_ref[...]) * (LANES ** -0.5)
    _rec_core(qq, kk, logf, i_ref, g_ref[...], gn_ref[...], st_ref, o_ref, chunk=chunk, n_groups=4, hpg=1)

    @pl.when(ci == pl.num_programs(1) - 1)
    def _():
        _state_out(st_ref, sout_ref, 4, 1)


def _rec_b_kernel(q_ref, k_ref, v_ref, g_ref, r_ref, wgk_ref, bgk_ref, gn_ref, s0_ref, o_ref, sout_ref, st_ref,
                  *, chunk):
    ci = pl.program_id(1)

    @pl.when(ci == 0)
    def _():
        _state_in(s0_ref, st_ref, 2, 2)

    gk = _dot(r_ref[...].astype(BF16), wgk_ref[...]) + bgk_ref[...]
    logf = _log_sigmoid(gk) * (1.0 / GK_NORM)
    qq = q_ref[...] * ((LANES // 2) ** -0.5)
    _rec_core(qq, k_ref[...], logf, v_ref, g_ref[...], gn_ref[...], st_ref, o_ref, chunk=chunk, n_groups=2, hpg=2)

    @pl.when(ci == pl.num_programs(1) - 1)
    def _():
        _state_out(st_ref, sout_ref, 2, 2)


def _rec_a(y, lb_gamma, gn_a, s0, l, chunk):
    b, L, _ = y.shape
    depth = lb_gamma.shape[0]
    w = 4 * LANES
    col = lambda k: pl.BlockSpec((None, chunk, w), lambda bi, ci: (bi, ci, k))
    return pl.pallas_call(
        functools.partial(_rec_a_kernel, l=l, chunk=chunk),
        grid=(b, L // chunk),
        in_specs=[
            col(0), col(1), col(2), col(3),
            pl.BlockSpec((depth, w), lambda bi, ci: (0, 0)),
            pl.BlockSpec((None, 1, LANES), lambda bi, ci: (l, 0, 0)),
            pl.BlockSpec((None, 4, LANES, LANES), lambda bi, ci: (bi, 0, 0, 0)),
        ],
        out_specs=[
            pl.BlockSpec((None, chunk, w), lambda bi, ci: (bi, ci, 0)),
            pl.BlockSpec((None, 4, LANES, LANES), lambda bi, ci: (bi, 0, 0, 0)),
        ],
        out_shape=[jax.ShapeDtypeStruct((b, L, w), BF16), jax.ShapeDtypeStruct(s0.shape, F32)],
        scratch_shapes=[pltpu.VMEM((4, LANES, LANES), F32)],
        compiler_params=_cparams(("parallel", "arbitrary")),
        name="rec_hgrn",
    )(y, y, y, y, lb_gamma, gn_a.reshape(depth, 1, LANES), s0)


def _rec_b(y, r, wgk_p, b_gk, gn_b, s0, l, chunk, col_b):
    b, L, _ = y.shape
    depth = gn_b.shape[0]
    c256 = col_b // 256
    c512 = (col_b + 512) // 512
    return pl.pallas_call(
        functools.partial(_rec_b_kernel, chunk=chunk),
        grid=(b, L // chunk),
        in_specs=[
            pl.BlockSpec((None, chunk, 256), lambda bi, ci: (bi, ci, c256)),
            pl.BlockSpec((None, chunk, 256), lambda bi, ci: (bi, ci, c256 + 1)),
            pl.BlockSpec((None, chunk, 512), lambda bi, ci: (bi, ci, c512)),
            pl.BlockSpec((None, chunk, 512), lambda bi, ci: (bi, ci, c512 + 1)),
            pl.BlockSpec((None, chunk, LANES), lambda bi, ci: (bi, ci, 0)),
            pl.BlockSpec((None, LANES, 256), lambda bi, ci: (l, 0, 0)),
            pl.BlockSpec((None, 1, 256), lambda bi, ci: (l, 0, 0)),
            pl.BlockSpec((None, 1, LANES), lambda bi, ci: (l, 0, 0)),
            pl.BlockSpec((None, 4, 64, LANES), lambda bi, ci: (bi, 0, 0, 0)),
        ],
        out_specs=[
            pl.BlockSpec((None, chunk, 512), lambda bi, ci: (bi, ci, 0)),
            pl.BlockSpec((None, 4, 64, LANES), lambda bi, ci: (bi, 0, 0, 0)),
        ],
        out_shape=[jax.ShapeDtypeStruct((b, L, 512), BF16), jax.ShapeDtypeStruct(s0.shape, F32)],
        scratch_shapes=[pltpu.VMEM((2, LANES, LANES), F32)],
        compiler_params=_cparams(("parallel", "arbitrary")),
        name="rec_gla",
    )(y, y, y, y, r, wgk_p, b_gk.reshape(depth, 1, 256), gn_b.reshape(depth, 1, LANES), s0)


def _dec_finish(s_new, q_col, g_row, gn, o_ref, h):
    o = jnp.sum(q_col * s_new, axis=0, keepdims=True)
    y = o * lax.rsqrt(jnp.mean(o * o, axis=-1, keepdims=True) + EPS) * gn
    o_ref[h] = y * _silu(g_row)


def _dec_a_kernel(q_ref, f_ref, i_ref, g_ref, lbg_ref, gn_ref, s0_ref, o_ref, sout_ref, *, l):
    lb_all = _lb_from_gamma([lbg_ref[i] for i in range(lbg_ref.shape[0])], l)
    for h in range(4):
        lb = lb_all[h]
        f = f_ref[h]
        logf = _logaddexp(jnp.log(lb), jnp.log1p(-lb) + _log_sigmoid(f))
        kk = (1.0 - lb) * jax.nn.sigmoid(-f)
        qq = _silu(q_ref[h]) * (LANES ** -0.5)
        s_new = jnp.exp(logf) * s0_ref[h] + kk * i_ref[h]
        sout_ref[h] = s_new
        _dec_finish(s_new, qq, g_ref[h], gn_ref[...], o_ref, h)


def _dec_b_kernel(q_ref, k_ref, v_ref, g_ref, r_ref, wgk_ref, bgk_ref, gn_ref, s0_ref, o_ref, sout_ref):
    r = r_ref[...]
    for h in range(4):
        gk = jnp.sum(wgk_ref[h] * r, axis=-1, keepdims=True) + bgk_ref[h]
        logf = _log_sigmoid(gk) * (1.0 / GK_NORM)
        qq = q_ref[h] * ((LANES // 2) ** -0.5)
        s_new = jnp.exp(logf) * s0_ref[h] + k_ref[h] * v_ref[h]
        sout_ref[h] = s_new
        _dec_finish(s_new, qq, g_ref[h], gn_ref[...], o_ref, h)


def _dec_a(qa, fa, ia, ga, lb_gamma, gn_a, s0, l):
    nb = qa.shape[0]
    depth = lb_gamma.shape[0]
    colspec = pl.BlockSpec((None, 4, LANES, 1), lambda b: (b, 0, 0, 0))
    rowspec = pl.BlockSpec((None, 4, 1, LANES), lambda b: (b, 0, 0, 0))
    stspec = pl.BlockSpec((None, 4, LANES, LANES), lambda b: (b, 0, 0, 0))
    o, s = pl.pallas_call(
        functools.partial(_dec_a_kernel, l=l),
        grid=(nb,),
        in_specs=[colspec, colspec, rowspec, rowspec,
                  pl.BlockSpec((depth, 4, LANES, 1), lambda b: (0, 0, 0, 0)),
                  pl.BlockSpec((None, 1, LANES), lambda b: (l, 0, 0)),
                  stspec],
        out_specs=[rowspec, stspec],
        out_shape=[jax.ShapeDtypeStruct((nb, 4, 1, LANES), F32), jax.ShapeDtypeStruct(s0.shape, F32)],
        compiler_params=_cparams(("parallel",)),
        name="dec_hgrn",
    )(qa.reshape(nb, 4, LANES, 1), fa.reshape(nb, 4, LANES, 1), ia.reshape(nb, 4, 1, LANES),
      ga.reshape(nb, 4, 1, LANES), lb_gamma.reshape(depth, 4, LANES, 1), gn_a.reshape(depth, 1, LANES), s0)
    return o.reshape(nb, 1, 4 * LANES).astype(BF16), s


def _dec_b(qb, kb, vb, gb, rb, wgk_t, b_gk, gn_b, s0, l):
    nb = qb.shape[0]
    depth = gn_b.shape[0]
    rank = rb.shape[-1]
    colspec = pl.BlockSpec((None, 4, 64, 1), lambda b: (b, 0, 0, 0))
    rowspec = pl.BlockSpec((None, 4, 1, LANES), lambda b: (b, 0, 0, 0))
    stspec = pl.BlockSpec((None, 4, 64, LANES), lambda b: (b, 0, 0, 0))
    o, s = pl.pallas_call(
        _dec_b_kernel,
        grid=(nb,),
        in_specs=[colspec, colspec, rowspec, rowspec,
                  pl.BlockSpec((None, 1, rank), lambda b: (b, 0, 0)),
                  pl.BlockSpec((None, 4, 64, rank), lambda b: (l, 0, 0, 0)),
                  pl.BlockSpec((None, 4, 64, 1), lambda b: (l, 0, 0, 0)),
                  pl.BlockSpec((None, 1, LANES), lambda b: (l, 0, 0)),
                  stspec],
        out_specs=[rowspec, stspec],
        out_shape=[jax.ShapeDtypeStruct((nb, 4, 1, LANES), F32), jax.ShapeDtypeStruct(s0.shape, F32)],
        compiler_params=_cparams(("parallel",)),
        name="dec_gla",
    )(qb.reshape(nb, 4, 64, 1), kb.reshape(nb, 4, 64, 1), vb.reshape(nb, 4, 1, LANES),
      gb.reshape(nb, 4, 1, LANES), rb.reshape(nb, 1, rank), wgk_t, b_gk.reshape(depth, 4, 64, 1),
      gn_b.reshape(depth, 1, LANES), s0)
    return o.reshape(nb, 1, 4 * LANES).astype(BF16), s


def _suffix_matrix(n, rows_first):
    a = lax.broadcasted_iota(jnp.int32, (n, n), 0)
    b = lax.broadcasted_iota(jnp.int32, (n, n), 1)
    if rows_first:
        m = jnp.where(a >= b, 1.0, 0.0).astype(BF16)
        return jnp.concatenate([m, m], axis=0)
    m = jnp.where(b >= a, 1.0, 0.0).astype(BF16)
    return jnp.concatenate([m, m], axis=1)


def _sb_prompt_kernel(bias_ref, qa_ref, qb_ref, k_ref, v_ref, oa_ref, ob_ref,
                      kbf_ref, vbf_ref, uu_ref, z_ref, sp_ref, suf_ref, *, l, tq, nq, scale, unroll):
    h = pl.program_id(1)
    i = pl.program_id(2)
    n_off = nq - 1

    @pl.when(i == 0)
    def _():
        kbf_ref[...] = k_ref[...].astype(BF16)
        vbf_ref[...] = v_ref[...].astype(BF16)
        uu_ref[...] = _suffix_matrix(tq, rows_first=True)

    bias = bias_ref[l, h]
    qa = qa_ref[...].astype(BF16)
    qb = qb_ref[...].astype(BF16)
    uu = uu_ref[...]
    ti = lax.broadcasted_iota(jnp.int32, (tq, tq), 0)
    si = lax.broadcasted_iota(jnp.int32, (tq, tq), 1)
    causal = si < ti

    def off_kb(s):
        return jnp.where(s < i, i - 1 - s, nq - 2 - s)

    def logits(slot, q, kb, masked):
        start = pl.multiple_of(kb * tq, tq)
        z = _dot_nt(q, kbf_ref[pl.ds(start, tq), :]) * scale + bias
        sp = jnp.maximum(z, 0.0) + jnp.log(1.0 + jnp.exp(-jnp.abs(z)))
        if masked:
            sp = jnp.where(causal, sp, 0.0)
        hi = sp.astype(BF16)
        z_ref[slot] = z
        sp_ref[slot, :, 0:tq] = hi
        sp_ref[slot, :, tq:2 * tq] = (sp - hi.astype(F32)).astype(BF16)

    def suffix(slot):
        suf_ref[slot] = _dot(sp_ref[slot], uu)

    def weigh(slot, kb, run, acc, masked):
        start = pl.multiple_of(kb * tq, tq)
        suf = suf_ref[slot]
        w = jnp.exp(z_ref[slot] - suf - run)
        if masked:
            w = jnp.where(causal, w, 0.0)
        acc = acc + _dot(w.astype(BF16), vbf_ref[pl.ds(start, tq), :])
        return run + suf[:, 0:1], acc

    def sweep(fn):
        def body(j, c):
            for u in range(unroll):
                fn(unroll * j + u)
            return c
        lax.fori_loop(0, n_off // unroll, body, 0)

    logits(0, qa, i, True)
    logits(1, qb, nq - 1 - i, True)
    sweep(lambda s: logits(2 + s, jnp.where(s < i, qa, qb), off_kb(s), False))
    suffix(0)
    suffix(1)
    sweep(lambda s: suffix(2 + s))

    run0 = jnp.zeros((tq, 1), F32)
    acc0 = jnp.zeros((tq, LANES), F32)
    run_a, acc_a = weigh(0, i, run0, acc0, True)
    run_b, acc_b = weigh(1, nq - 1 - i, run0, acc0, True)

    def body3(j, c):
        run, acc, out_a = c
        for u in range(unroll):
            s = unroll * j + u
            at_b = s == i
            out_a = jnp.where(at_b, acc, out_a)
            run = jnp.where(at_b, run_b, run)
            acc = jnp.where(at_b, acc_b, acc)
            run, acc = weigh(2 + s, off_kb(s), run, acc, False)
        return run, acc, out_a

    _, acc, out_a = lax.fori_loop(0, n_off // unroll, body3, (run_a, acc_a, acc_a))
    oa_ref[...] = out_a.astype(oa_ref.dtype)
    ob_ref[...] = acc.astype(ob_ref.dtype)


def _sb_prompt(y, b_sb, l, col_q, n_heads, tq):
    b, L, _ = y.shape
    cq = col_q // LANES
    nq = L // tq
    n_off = nq - 1
    assert nq % 2 == 0
    unroll = max(u for u in (1, 2, 3, 4, 5) if n_off % u == 0)
    half = nq // 2
    n_slots = nq + 1
    o_lo, o_hi = pl.pallas_call(
        functools.partial(_sb_prompt_kernel, l=l, tq=tq, nq=nq, scale=LANES ** -0.5, unroll=unroll),
        grid=(b, n_heads, half),
        in_specs=[
            pl.BlockSpec(memory_space=pltpu.SMEM),
            pl.BlockSpec((None, tq, LANES), lambda bi, h, i: (bi, i, cq + h)),
            pl.BlockSpec((None, tq, LANES), lambda bi, h, i: (bi, nq - 1 - i, cq + h)),
            pl.BlockSpec((None, L, LANES), lambda bi, h, i: (bi, 0, cq + n_heads + h)),
            pl.BlockSpec((None, L, LANES), lambda bi, h, i: (bi, 0, cq + 2 * n_heads + h)),
        ],
        out_specs=[pl.BlockSpec((None, tq, LANES), lambda bi, h, i: (bi, i, h)),
                   pl.BlockSpec((None, tq, LANES), lambda bi, h, i: (bi, half - 1 - i, h))],
        out_shape=[jax.ShapeDtypeStruct((b, L // 2, n_heads * LANES), BF16)] * 2,
        scratch_shapes=[pltpu.VMEM((L, LANES), BF16), pltpu.VMEM((L, LANES), BF16),
                        pltpu.VMEM((2 * tq, tq), BF16),
                        pltpu.VMEM((n_slots, tq, tq), F32), pltpu.VMEM((n_slots, tq, 2 * tq), BF16),
                        pltpu.VMEM((n_slots, tq, tq), F32)],
        compiler_params=_cparams(("parallel", "parallel", "arbitrary")),
        name="sb_prompt",
    )(b_sb, y, y, y, y)
    return jnp.concatenate([o_lo, o_hi], axis=1)


def _sb_decode_kernel(pt_ref, bias_ref, q_ref, *refs, page, npg, scale):
    k_refs, v_refs = refs[:npg], refs[npg:2 * npg]
    o_ref, run_ref, acc_ref = refs[2 * npg:]
    p = pl.program_id(1)

    @pl.when(p == 0)
    def _():
        run_ref[...] = jnp.zeros_like(run_ref)
        acc_ref[...] = jnp.zeros_like(acc_ref)

    q = q_ref[...]
    bias = bias_ref[...]
    nh = q.shape[0]
    tok = lax.broadcasted_iota(jnp.int32, (page, nh, LANES), 0)
    lane = lax.broadcasted_iota(jnp.int32, (page, nh, LANES), 2)
    own_lane = tok == lane
    ones = jnp.ones((LANES, LANES), BF16)
    uu = _suffix_matrix(page, rows_first=True)
    run = run_ref[...]
    acc = acc_ref[...]
    for i in range(npg):
        prod = (k_refs[i][...] * q[None]).reshape(page * nh, LANES).astype(BF16)
        zrep = _dot(prod, ones).reshape(page, nh, LANES)
        z = jnp.sum(jnp.where(own_lane, zrep, 0.0), axis=0) * scale + bias
        lk = -(jnp.maximum(z, 0.0) + jnp.log(1.0 + jnp.exp(-jnp.abs(z))))
        hi = lk.astype(BF16)
        lo = (lk - hi.astype(F32)).astype(BF16)
        suf = _dot(jnp.concatenate([hi, lo], axis=1), uu)
        w = jnp.exp(z + suf + run)
        wsel = jnp.where(own_lane, w[None], 0.0).reshape(page * nh, LANES).astype(BF16)
        wrep = _dot(wsel, ones).reshape(page, nh, LANES)
        acc = acc + jnp.sum(wrep * v_refs[i][...], axis=0)
        run = run + suf[:, 0:1]
    run_ref[...] = run
    acc_ref[...] = acc

    @pl.when(p == pl.num_programs(1) - 1)
    def _():
        o_ref[...] = acc


def _sb_decode(q, cache_k, cache_v, page_table, b_sb, l):
    nb = q.shape[0]
    _, _, page, nh, dh = cache_k.shape
    n_pages = page_table.shape[1]
    npg = math.gcd(DEC_PAGES, n_pages)
    assert nh == SUBLANES and dh == LANES and page == LANES
    bias = jnp.broadcast_to(b_sb[l][:, None], (nh, LANES))

    def page_spec(i):
        return pl.BlockSpec((None, None, page, nh, dh),
                            lambda b, p, pt: (l, pt[b, n_pages - 1 - (p * npg + i)], 0, 0, 0))

    grid_spec = pltpu.PrefetchScalarGridSpec(
        num_scalar_prefetch=1,
        grid=(nb, n_pages // npg),
        in_specs=[pl.BlockSpec((nh, LANES), lambda b, p, pt: (0, 0)),
                  pl.BlockSpec((None, nh, dh), lambda b, p, pt: (b, 0, 0))]
                 + [page_spec(i) for i in range(npg)] * 2,
        out_specs=pl.BlockSpec((None, nh, dh), lambda b, p, pt: (b, 0, 0)),
        scratch_shapes=[pltpu.VMEM((nh, LANES), F32), pltpu.VMEM((nh, dh), F32)],
    )
    o = pl.pallas_call(
        functools.partial(_sb_decode_kernel, page=page, npg=npg, scale=dh ** -0.5),
        grid_spec=grid_spec,
        out_shape=jax.ShapeDtypeStruct((nb, nh, dh), F32),
        compiler_params=_cparams(("parallel", "arbitrary")),
        name="sb_decode",
    )(page_table, bias, q.reshape(nb, nh, dh), *([cache_k] * npg), *([cache_v] * npg))
    return o.reshape(nb, 1, nh * dh).astype(BF16)


def _trunk(path, x, mod_all, wts, state_a, state_b, cache, dims):
    d_model = x.shape[-1]
    depth = wts["w_o"].shape[0]
    c1, c2 = dims["col_rec"], dims["col_att"]
    is_prompt = cache is None
    tm = path.tm
    ks, vs, sas, sbs = [], [], [], []
    for l in range(depth):
        h = _norm_mod(path, x, wts["g_mix"], mod_all, l, 1, 0)
        y1 = _mm(path, h, wts["w_in_rec"], l, 0, c1["end"], c1["end"] // 2, name="in_proj_rec")
        y2 = _mm(path, h, wts["w_in_att"], l, 0, c2["end"], 1024, name="in_proj_att")
        r = _mm(path, h, wts["w_in_rb"], l, 0, LANES, LANES, name="in_proj_r")
        bv, lv, _ = y1.shape
        k_new = y2[:, :, c2["kc"]:c2["vc"]]
        v_new = y2[:, :, c2["vc"]:c2["ma"]]
        if is_prompt:
            o_a, s_a = _rec_a(y1, wts["lb_gamma"], wts["gn_a"], state_a[l], l, dims["chunk"])
            o_b, s_b = _rec_b(y1, r, wts["w_gk_p"], wts["b_gk"], wts["gn_b"], state_b[l], l, dims["chunk"], c1["qb"])
            o_c = _sb_prompt(y2, wts["b_sb"], l, c2["qc"], dims["h_c"], dims["tq"])
        else:
            f1, f2 = y1.reshape(lv, -1), y2.reshape(lv, -1)
            s1 = lambda a, b_: f1[:, c1[a]:c1[b_]]
            o_a, s_a = _dec_a(s1("qa", "fa"), s1("fa", "ia"), s1("ia", "ga"), s1("ga", "qb"),
                              wts["lb_gamma"], wts["gn_a"], state_a[l], l)
            o_b, s_b = _dec_b(s1("qb", "kb"), s1("kb", "vb"), s1("vb", "gb"), s1("gb", "end"),
                              r.reshape(lv, -1)[:, :dims["gk_rank"]], wts["w_gk_t"], wts["b_gk"], wts["gn_b"],
                              state_b[l], l)
            o_c = _sb_decode(f2[:, c2["qc"]:c2["kc"]], cache[0], cache[1], cache[2], wts["b_sb"], l)
            o_a, o_b, o_c = (o.reshape(bv, lv, -1) for o in (o_a, o_b, o_c))
        merged = _merge(path, o_a, o_b, o_c, y2, c2["ma"], wts["w_br"], l, 512)
        x = _mm_resid(path, merged, wts["w_o"], x, mod_all, l, 2, 1024, name="out_proj")
        h2 = _norm_mod(path, x, wts["g_ffn"], mod_all, l, 4, 3)
        act = _swiglu(path, h2, wts["w_gu"], l, 512)
        x = _mm_resid(path, act, wts["w_down"], x, mod_all, l, 5, 512, name="down_proj")
        ks.append(k_new)
        vs.append(v_new)
        sas.append(s_a)
        sbs.append(s_b)
    xf = _final_norm(x, wts["g_final"], min(tm, 512))
    return xf, jnp.stack(ks), jnp.stack(vs), jnp.stack(sas), jnp.stack(sbs)


def kernel(x_prompt, x_sample, cache_k, cache_v, state_hgrn, state_gla, page_table, c_prompt, c_sample, w_ada, b_ada, g_mix, w_in, b_sb, lb_gamma, gn_a, w_gk, b_gk, gn_b, w_br, w_o, g_ffn, w_gu, w_down, g_final):
    bp, seq, d_model = x_prompt.shape
    nb = x_sample.shape[0]
    depth, _, _, h_c, dh_c = cache_k.shape[0], None, None, cache_k.shape[3], cache_k.shape[4]
    h_a, dk_a, dv_a = state_hgrn.shape[2:]
    h_b, dk_b, dv_b = state_gla.shape[2:]
    gk_rank = w_gk.shape[1]
    w_a, w_b, w_c = h_a * dv_a, h_b * dv_b, h_c * dh_c

    def offsets(widths):
        col, off = {}, 0
        for name, wd in widths:
            col[name] = off
            off += wd
        col["end"] = off
        return col

    col_rec = offsets([("qa", h_a * dk_a), ("fa", h_a * dk_a), ("ia", w_a), ("ga", w_a),
                       ("qb", h_b * dk_b), ("kb", h_b * dk_b), ("vb", w_b), ("gb", w_b)])
    col_att = offsets([("qc", w_c), ("kc", w_c), ("vc", w_c), ("ma", d_model), ("mb", d_model), ("mc", d_model)])
    src_rb = col_rec["end"]
    w_in_rb = jnp.concatenate([w_in[:, :, src_rb:src_rb + gk_rank],
                               jnp.zeros((depth, d_model, LANES - gk_rank), w_in.dtype)], axis=-1).astype(BF16)
    w_gk_p = jnp.concatenate([w_gk, jnp.zeros((depth, LANES - gk_rank, w_gk.shape[-1]), w_gk.dtype)],
                             axis=1).astype(BF16)
    wts = {
        "w_in_rec": w_in[:, :, :src_rb].astype(BF16), "w_in_att": w_in[:, :, src_rb + gk_rank:].astype(BF16),
        "w_in_rb": w_in_rb,
        "w_br": w_br.astype(BF16), "w_o": w_o.astype(BF16), "w_gu": w_gu.astype(BF16),
        "w_down": w_down.astype(BF16), "w_gk_p": w_gk_p,
        "w_gk_t": jnp.swapaxes(w_gk, 1, 2).reshape(depth, h_b, dk_b, gk_rank),
        "g_mix": g_mix, "g_ffn": g_ffn, "g_final": g_final, "b_sb": b_sb, "lb_gamma": lb_gamma,
        "gn_a": gn_a, "gn_b": gn_b, "b_gk": b_gk,
    }
    dims = {"col_rec": col_rec, "col_att": col_att, "chunk": 256, "tq": 256, "h_c": h_c, "gk_rank": gk_rank}

    c_all = jnp.concatenate([c_sample, c_prompt, jnp.zeros((16 - nb - bp, d_model), F32)], axis=0)
    mod_all = _ada_mod(c_all, w_ada, b_ada)

    prompt = _Path(bp, seq, min(1024, seq), per_row_mod=False, mod_row0=nb)
    sample = _Path(1, nb, nb, per_row_mod=True, mod_row0=0)

    zeros_a = jnp.zeros((depth, bp) + state_hgrn.shape[2:], state_hgrn.dtype)
    zeros_b = jnp.zeros((depth, bp) + state_gla.shape[2:], state_gla.dtype)
    y_p, k_p, v_p, sa_p, sb_p = _trunk(prompt, x_prompt, mod_all, wts, zeros_a, zeros_b, None, dims)

    cache = (cache_k, cache_v, page_table)
    y_s, k_s, v_s, sa_s, sb_s = _trunk(sample, x_sample.reshape(1, nb, d_model), mod_all, wts,
                                       state_hgrn, state_gla, cache, dims)

    return (y_p, y_s.reshape(nb, 1, d_model),
            k_p.reshape(depth, bp, seq, h_c, dh_c), v_p.reshape(depth, bp, seq, h_c, dh_c),
            k_s.reshape(depth, nb, 1, h_c, dh_c), v_s.reshape(depth, nb, 1, h_c, dh_c),
            sa_p, sa_s, sb_p, sb_s)
```

```python
import functools
import math

import jax
import jax.numpy as jnp
from jax import lax
from jax.experimental import pallas as pl
from jax.experimental.pallas import tpu as pltpu

F32 = jnp.float32
BF16 = jnp.bfloat16

EPS = 1e-6
GK_NORM = 16.0
N_MOD = 6
LANES = 128
SUBLANES = 8
VMEM_LIMIT = 56 * 1024 * 1024
DEC_PAGES = 16


def _cparams(sem):
    return pltpu.CompilerParams(dimension_semantics=sem, vmem_limit_bytes=VMEM_LIMIT)


def _silu(x):
    return x * jax.nn.sigmoid(x)


def _log_sigmoid(x):
    return jnp.minimum(x, 0.0) - jnp.log1p(jnp.exp(-jnp.abs(x)))


def _logaddexp(a, b):
    m = jnp.maximum(a, b)
    return m + jnp.log1p(jnp.exp(-jnp.abs(a - b)))


def _dot(a, b):
    return jnp.dot(a, b, preferred_element_type=F32)


def _dot_nt(a, b):
    return lax.dot_general(a, b, (((1,), (1,)), ((), ())), preferred_element_type=F32)


def _dot_tn(a, b):
    return lax.dot_general(a, b, (((0,), (0,)), ((), ())), preferred_element_type=F32)


class _Path:
    def __init__(self, bv, lv, tm, per_row_mod, mod_row0):
        self.bv, self.lv, self.tm = bv, lv, tm
        self.per_row_mod = per_row_mod
        self.mod_row0 = mod_row0

    def grid_rows(self):
        return (self.bv, self.lv // self.tm)

    def mod_operand(self, mod_all):
        if self.per_row_mod:
            return mod_all
        d, r, w = mod_all.shape
        return mod_all.reshape(d, r, 1, w)

    def mod_spec(self, l, k, d_model, tn=None, with_j=False):
        tn = d_model if tn is None else tn
        per = d_model // tn
        if self.per_row_mod:
            if with_j:
                return pl.BlockSpec((None, self.tm, tn), lambda b, i, j: (l, i, k * per + j))
            return pl.BlockSpec((None, self.tm, tn), lambda b, i: (l, i, k * per))
        r0 = self.mod_row0
        if with_j:
            return pl.BlockSpec((None, None, 1, tn), lambda b, i, j: (l, r0 + b, 0, k * per + j))
        return pl.BlockSpec((None, None, 1, tn), lambda b, i: (l, r0 + b, 0, k * per))


def _ada_kernel(c_ref, w_ref, b_ref, o_ref):
    a = _silu(c_ref[...]).astype(BF16)
    o_ref[...] = _dot(a, w_ref[...].astype(BF16)) + b_ref[...]


def _ada_mod(c_all, w_ada, b_ada, tn=1024):
    depth, d, n = w_ada.shape
    rows = c_all.shape[0]
    return pl.pallas_call(
        _ada_kernel,
        grid=(depth, n // tn),
        in_specs=[
            pl.BlockSpec((rows, d), lambda l, j: (0, 0)),
            pl.BlockSpec((None, d, tn), lambda l, j: (l, 0, j)),
            pl.BlockSpec((None, 1, tn), lambda l, j: (l, 0, j)),
        ],
        out_specs=pl.BlockSpec((None, rows, tn), lambda l, j: (l, 0, j)),
        out_shape=jax.ShapeDtypeStruct((depth, rows, n), F32),
        compiler_params=_cparams(("parallel", "parallel")),
        name="ada_mod",
    )(c_all, w_ada, b_ada.reshape(depth, 1, n))


def _norm_mod_kernel(x_ref, g_ref, sc_ref, sh_ref, o_ref):
    x = x_ref[...]
    y = x * lax.rsqrt(jnp.mean(x * x, axis=-1, keepdims=True) + EPS) * g_ref[...]
    o_ref[...] = (y * (1.0 + sc_ref[...]) + sh_ref[...]).astype(o_ref.dtype)


def _norm_kernel(x_ref, g_ref, o_ref):
    x = x_ref[...]
    y = x * lax.rsqrt(jnp.mean(x * x, axis=-1, keepdims=True) + EPS) * g_ref[...]
    o_ref[...] = y.astype(o_ref.dtype)


def _norm_mod(path, x, g, mod_all, l, k_sc, k_sh, tm=None):
    bv, lv, d = x.shape
    tm = path.tm if tm is None else tm
    p = _Path(bv, lv, tm, path.per_row_mod, path.mod_row0)
    depth = g.shape[0]
    mod = p.mod_operand(mod_all)
    return pl.pallas_call(
        _norm_mod_kernel,
        grid=p.grid_rows(),
        in_specs=[
            pl.BlockSpec((None, tm, d), lambda b, i: (b, i, 0)),
            pl.BlockSpec((None, 1, d), lambda b, i: (l, 0, 0)),
            p.mod_spec(l, k_sc, d),
            p.mod_spec(l, k_sh, d),
        ],
        out_specs=pl.BlockSpec((None, tm, d), lambda b, i: (b, i, 0)),
        out_shape=jax.ShapeDtypeStruct((bv, lv, d), BF16),
        compiler_params=_cparams(("parallel", "parallel")),
        name="norm_mod",
    )(x, g.reshape(depth, 1, d), mod, mod)


def _final_norm(x, g, tm):
    bv, lv, d = x.shape
    return pl.pallas_call(
        _norm_kernel,
        grid=(bv, lv // tm),
        in_specs=[
            pl.BlockSpec((None, tm, d), lambda b, i: (b, i, 0)),
            pl.BlockSpec((1, d), lambda b, i: (0, 0)),
        ],
        out_specs=pl.BlockSpec((None, tm, d), lambda b, i: (b, i, 0)),
        out_shape=jax.ShapeDtypeStruct((bv, lv, d), F32),
        compiler_params=_cparams(("parallel", "parallel")),
        name="final_norm",
    )(x, g.reshape(1, d))


def _mm_nt_kernel(a_ref, w_ref, o_ref):
    o_ref[...] = _dot_nt(a_ref[...], w_ref[...]).astype(o_ref.dtype)


def _mm_nt(path, a, w_t, l, tn, tm=None, out_dtype=F32, name="mm"):
    bv, lv, k = a.shape
    ncols = w_t.shape[1]
    tm = path.tm if tm is None else tm
    assert ncols % tn == 0 and lv % tm == 0
    return pl.pallas_call(
        _mm_nt_kernel,
        grid=(bv, lv // tm, ncols // tn),
        in_specs=[
            pl.BlockSpec((None, tm, k), lambda b, i, j: (b, i, 0)),
            pl.BlockSpec((None, tn, k), lambda b, i, j: (l, j, 0)),
        ],
        out_specs=pl.BlockSpec((None, tm, tn), lambda b, i, j: (b, i, j)),
        out_shape=jax.ShapeDtypeStruct((bv, lv, ncols), out_dtype),
        compiler_params=_cparams(("parallel", "parallel", "arbitrary")),
        name=name,
    )(a, w_t)


def _mm_resid_kernel(a_ref, w_ref, x_ref, gt_ref, o_ref):
    o_ref[...] = x_ref[...] + gt_ref[...] * _dot(a_ref[...], w_ref[...])


def _mm_resid(path, a, w, x, mod_all, l, k_gt, tn, tm=None, name="mm_resid"):
    bv, lv, k = a.shape
    d = w.shape[-1]
    tm = path.tm if tm is None else tm
    p = _Path(bv, lv, tm, path.per_row_mod, path.mod_row0)
    return pl.pallas_call(
        _mm_resid_kernel,
        grid=(bv, lv // tm, d // tn),
        in_specs=[
            pl.BlockSpec((None, tm, k), lambda b, i, j: (b, i, 0)),
            pl.BlockSpec((None, k, tn), lambda b, i, j: (l, 0, j)),
            pl.BlockSpec((None, tm, tn), lambda b, i, j: (b, i, j)),
            p.mod_spec(l, k_gt, d, tn=tn, with_j=True),
        ],
        out_specs=pl.BlockSpec((None, tm, tn), lambda b, i, j: (b, i, j)),
        out_shape=jax.ShapeDtypeStruct((bv, lv, d), F32),
        compiler_params=_cparams(("parallel", "parallel", "arbitrary")),
        name=name,
    )(a, w, x, p.mod_operand(mod_all))


def _swiglu_kernel(a_ref, wg_ref, wu_ref, o_ref):
    a = a_ref[...]
    o_ref[...] = (_silu(_dot(a, wg_ref[...])) * _dot(a, wu_ref[...])).astype(o_ref.dtype)


def _swiglu(path, a, w_gu, l, tn, tm=None):
    bv, lv, k = a.shape
    d_ff = w_gu.shape[-1] // 2
    tm = path.tm if tm is None else tm
    nj = d_ff // tn
    assert d_ff % tn == 0
    return pl.pallas_call(
        _swiglu_kernel,
        grid=(bv, lv // tm, nj),
        in_specs=[
            pl.BlockSpec((None, tm, k), lambda b, i, j: (b, i, 0)),
            pl.BlockSpec((None, k, tn), lambda b, i, j: (l, 0, j)),
            pl.BlockSpec((None, k, tn), lambda b, i, j: (l, 0, nj + j)),
        ],
        out_specs=pl.BlockSpec((None, tm, tn), lambda b, i, j: (b, i, j)),
        out_shape=jax.ShapeDtypeStruct((bv, lv, d_ff), BF16),
        compiler_params=_cparams(("parallel", "parallel", "arbitrary")),
        name="swiglu",
    )(a, w_gu, w_gu)


def _merge_kernel(oa_ref, ob_ref, oc_ref, ma_ref, mb_ref, mc_ref, wa_ref, wb_ref, wc_ref, o_ref):
    m = jax.nn.sigmoid(ma_ref[...]) * _dot(oa_ref[...], wa_ref[...])
    m = m + jax.nn.sigmoid(mb_ref[...]) * _dot(ob_ref[...], wb_ref[...])
    m = m + jax.nn.sigmoid(mc_ref[...]) * _dot(oc_ref[...], wc_ref[...])
    o_ref[...] = m.astype(o_ref.dtype)


def _merge(path, o_a, o_b, o_c, y, gate_col0, w_br, l, tn, tm=None):
    bv, lv, wa = o_a.shape
    wb, wc = o_b.shape[-1], o_c.shape[-1]
    d = w_br.shape[-1]
    tm = path.tm if tm is None else tm
    g0 = gate_col0 // tn
    per = d // tn
    assert gate_col0 % tn == 0 and wb == wa and wc == 2 * wa
    return pl.pallas_call(
        _merge_kernel,
        grid=(bv, lv // tm, d // tn),
        in_specs=[
            pl.BlockSpec((None, tm, wa), lambda b, i, j: (b, i, 0)),
            pl.BlockSpec((None, tm, wb), lambda b, i, j: (b, i, 0)),
            pl.BlockSpec((None, tm, wc), lambda b, i, j: (b, i, 0)),
            pl.BlockSpec((None, tm, tn), lambda b, i, j: (b, i, g0 + j)),
            pl.BlockSpec((None, tm, tn), lambda b, i, j: (b, i, g0 + per + j)),
            pl.BlockSpec((None, tm, tn), lambda b, i, j: (b, i, g0 + 2 * per + j)),
            pl.BlockSpec((None, wa, tn), lambda b, i, j: (l, 0, j)),
            pl.BlockSpec((None, wb, tn), lambda b, i, j: (l, 1, j)),
            pl.BlockSpec((None, wc, tn), lambda b, i, j: (l, 1, j)),
        ],
        out_specs=pl.BlockSpec((None, tm, tn), lambda b, i, j: (b, i, j)),
        out_shape=jax.ShapeDtypeStruct((bv, lv, d), BF16),
        compiler_params=_cparams(("parallel", "parallel", "arbitrary")),
        name="merge",
    )(o_a, o_b, o_c, y, y, y, w_br, w_br, w_br)


def _lb_from_gamma(gam, l):
    depth = len(gam)
    m = gam[0]
    for i in range(1, depth):
        m = jnp.maximum(m, gam[i])
    e = [jnp.exp(gam[i] - m) for i in range(depth)]
    tot = e[0]
    for i in range(1, depth):
        tot = tot + e[i]
    sm = [ei / tot for ei in e]
    cs = [sm[0]]
    for i in range(1, depth):
        cs.append(cs[-1] + sm[i])
    return cs[l] - cs[0]


def _rec_core(qq, kk, logf, v_ref_tile, g_tile, gn, st_ref, o_ref, *, chunk, n_groups, hpg):
    c = chunk
    wq = n_groups * LANES
    dk = LANES // hpg
    n_levels = int(math.log2(c))
    row = lax.broadcasted_iota(jnp.int32, (c, wq), 0)
    ti = lax.broadcasted_iota(jnp.int32, (c, c), 0)
    si = lax.broadcasted_iota(jnp.int32, (c, c), 1)
    x = jnp.bitwise_xor(ti, si)
    lvl = jnp.zeros((c, c), jnp.int32)
    for lev in range(1, n_levels + 1):
        lvl = lvl + jnp.where(x >= (1 << (lev - 1)), 1, 0)
    lvl = jnp.where(ti > si, lvl, -1)
    lane = lax.broadcasted_iota(jnp.int32, (c, LANES), 1)

    def head_mask(a, sub):
        if hpg == 1:
            return a
        return jnp.where(lane < dk, a, 0.0) if sub == 0 else jnp.where(lane >= dk, a, 0.0)

    pre = logf
    tot = logf
    ql, kl = [], []
    for lev in range(1, n_levels + 1):
        half = 1 << (lev - 1)
        upper = (row & half) != 0
        e = jnp.exp(jnp.where(upper, pre, tot - pre))
        ql.append(qq * e)
        kl.append(kk * e)
        up = pltpu.roll(tot, half, 0)
        dn = pltpu.roll(tot, c - half, 0)
        pre = pre + jnp.where(upper, up, 0.0)
        tot = tot + jnp.where(upper, up, dn)
    q_in = qq * jnp.exp(pre)
    k_out = kk * jnp.exp(tot - pre)
    d_all = jnp.exp(tot[0:1, :])
    qk = qq * kk

    for g in range(n_groups):
        gs = slice(g * LANES, (g + 1) * LANES)
        st = st_ref[g]
        st_bf = st.astype(BF16)
        st_new = st * d_all[:, gs]
        for sub in range(hpg):
            h = g * hpg + sub
            hs = slice(h * LANES, (h + 1) * LANES)
            v_bf = v_ref_tile[:, hs].astype(BF16)
            att = jnp.zeros((c, c), F32)
            for lev in range(1, n_levels + 1):
                p = _dot_nt(head_mask(ql[lev - 1][:, gs], sub).astype(BF16), kl[lev - 1][:, gs].astype(BF16))
                att = jnp.where(lvl == lev, p, att)
            diag = jnp.sum(head_mask(qk[:, gs], sub), axis=-1, keepdims=True)
            att = jnp.where(x == 0, diag, att)
            o = _dot_nt(head_mask(q_in[:, gs], sub).astype(BF16), st_bf) + _dot(att.astype(BF16), v_bf)
            st_new = st_new + _dot_tn(v_bf, head_mask(k_out[:, gs], sub).astype(BF16))
            y = o * lax.rsqrt(jnp.mean(o * o, axis=-1, keepdims=True) + EPS) * gn
            o_ref[:, hs] = (y * _silu(g_tile[:, hs])).astype(o_ref.dtype)
        st_ref[g] = st_new


def _state_in(s0_ref, st_ref, n_groups, hpg):
    dk = LANES // hpg
    for g in range(n_groups):
        blk = jnp.concatenate([s0_ref[g * hpg + sub] for sub in range(hpg)], axis=0) if hpg > 1 else s0_ref[g]
        st_ref[g] = blk.T


def _state_out(st_ref, sout_ref, n_groups, hpg):
    dk = LANES // hpg
    for g in range(n_groups):
        t = st_ref[g].T
        for sub in range(hpg):
            sout_ref[g * hpg + sub] = t[sub * dk:(sub + 1) * dk, :]


def _rec_a_kernel(q_ref, f_ref, i_ref, g_ref, lbg_ref, gn_ref, s0_ref, o_ref, sout_ref, st_ref, *, l, chunk):
    ci = pl.program_id(1)

    @pl.when(ci == 0)
    def _():
        _state_in(s0_ref, st_ref, 4, 1)

    lb = _lb_from_gamma([lbg_ref[i:i + 1, :] for i in range(lbg_ref.shape[0])], l)
    f = f_ref[...]
    logf = _logaddexp(jnp.log(lb), jnp.log1p(-lb) + _log_sigmoid(f))
    kk = (1.0 - lb) * jax.nn.sigmoid(-f)
    qq = _silu(q_ref[...]) * (LANES ** -0.5)
    _rec_core(qq, kk, logf, i_ref, g_ref[...], gn_ref[...], st_ref, o_ref, chunk=chunk, n_groups=4, hpg=1)

    @pl.when(ci == pl.num_programs(1) - 1)
    def _():
        _state_out(st_ref, sout_ref, 4, 1)


def _rec_b_kernel(q_ref, k_ref, v_ref, g_ref, r_ref, wgk_ref, bgk_ref, gn_ref, s0_ref, o_ref, sout_ref, st_ref,
                  *, chunk):
    ci = pl.program_id(1)

    @pl.when(ci == 0)
    def _():
        _state_in(s0_ref, st_ref, 2, 2)

    gk = _dot(r_ref[...].astype(BF16), wgk_ref[...]) + bgk_ref[...]
    logf = _log_sigmoid(gk) * (1.0 / GK_NORM)
    qq = q_ref[...] * ((LANES // 2) ** -0.5)
    _rec_core(qq, k_ref[...], logf, v_ref, g_ref[...], gn_ref[...], st_ref, o_ref, chunk=chunk, n_groups=2, hpg=2)

    @pl.when(ci == pl.num_programs(1) - 1)
    def _():
        _state_out(st_ref, sout_ref, 2, 2)


def _rec_a(y, lb_gamma, gn_a, s0, l, chunk):
    b, L, _ = y.shape
    depth = lb_gamma.shape[0]
    w = 4 * LANES
    col = lambda k: pl.BlockSpec((None, chunk, w), lambda bi, ci: (bi, ci, k))
    return pl.pallas_call(
        functools.partial(_rec_a_kernel, l=l, chunk=chunk),
        grid=(b, L // chunk),
        in_specs=[
            col(0), col(1), col(2), col(3),
            pl.BlockSpec((depth, w), lambda bi, ci: (0, 0)),
            pl.BlockSpec((None, 1, LANES), lambda bi, ci: (l, 0, 0)),
            pl.BlockSpec((None, 4, LANES, LANES), lambda bi, ci: (bi, 0, 0, 0)),
        ],
        out_specs=[
            pl.BlockSpec((None, chunk, w), lambda bi, ci: (bi, ci, 0)),
            pl.BlockSpec((None, 4, LANES, LANES), lambda bi, ci: (bi, 0, 0, 0)),
        ],
        out_shape=[jax.ShapeDtypeStruct((b, L, w), BF16), jax.ShapeDtypeStruct(s0.shape, F32)],
        scratch_shapes=[pltpu.VMEM((4, LANES, LANES), F32)],
        compiler_params=_cparams(("parallel", "arbitrary")),
        name="rec_hgrn",
    )(y, y, y, y, lb_gamma, gn_a.reshape(depth, 1, LANES), s0)


def _rec_b(y, r, wgk_p, b_gk, gn_b, s0, l, chunk, col_b):
    b, L, _ = y.shape
    depth = gn_b.shape[0]
    c256 = col_b // 256
    c512 = (col_b + 512) // 512
    return pl.pallas_call(
        functools.partial(_rec_b_kernel, chunk=chunk),
        grid=(b, L // chunk),
        in_specs=[
            pl.BlockSpec((None, chunk, 256), lambda bi, ci: (bi, ci, c256)),
            pl.BlockSpec((None, chunk, 256), lambda bi, ci: (bi, ci, c256 + 1)),
            pl.BlockSpec((None, chunk, 512), lambda bi, ci: (bi, ci, c512)),
            pl.BlockSpec((None, chunk, 512), lambda bi, ci: (bi, ci, c512 + 1)),
            pl.BlockSpec((None, chunk, LANES), lambda bi, ci: (bi, ci, 0)),
            pl.BlockSpec((None, LANES, 256), lambda bi, ci: (l, 0, 0)),
            pl.BlockSpec((None, 1, 256), lambda bi, ci: (l, 0, 0)),
            pl.BlockSpec((None, 1, LANES), lambda bi, ci: (l, 0, 0)),
            pl.BlockSpec((None, 4, 64, LANES), lambda bi, ci: (bi, 0, 0, 0)),
        ],
        out_specs=[
            pl.BlockSpec((None, chunk, 512), lambda bi, ci: (bi, ci, 0)),
            pl.BlockSpec((None, 4, 64, LANES), lambda bi, ci: (bi, 0, 0, 0)),
        ],
        out_shape=[jax.ShapeDtypeStruct((b, L, 512), BF16), jax.ShapeDtypeStruct(s0.shape, F32)],
        scratch_shapes=[pltpu.VMEM((2, LANES, LANES), F32)],
        compiler_params=_cparams(("parallel", "arbitrary")),
        name="rec_gla",
    )(y, y, y, y, r, wgk_p, b_gk.reshape(depth, 1, 256), gn_b.reshape(depth, 1, LANES), s0)


def _dec_finish(s_new, q_col, g_row, gn, o_ref, h):
    o = jnp.sum(q_col * s_new, axis=0, keepdims=True)
    y = o * lax.rsqrt(jnp.mean(o * o, axis=-1, keepdims=True) + EPS) * gn
    o_ref[h] = y * _silu(g_row)


def _dec_a_kernel(q_ref, f_ref, i_ref, g_ref, lbg_ref, gn_ref, s0_ref, o_ref, sout_ref, *, l):
    lb_all = _lb_from_gamma([lbg_ref[i] for i in range(lbg_ref.shape[0])], l)
    for h in range(4):
        lb = lb_all[h]
        f = f_ref[h]
        logf = _logaddexp(jnp.log(lb), jnp.log1p(-lb) + _log_sigmoid(f))
        kk = (1.0 - lb) * jax.nn.sigmoid(-f)
        qq = _silu(q_ref[h]) * (LANES ** -0.5)
        s_new = jnp.exp(logf) * s0_ref[h] + kk * i_ref[h]
        sout_ref[h] = s_new
        _dec_finish(s_new, qq, g_ref[h], gn_ref[...], o_ref, h)


def _dec_b_kernel(q_ref, k_ref, v_ref, g_ref, r_ref, wgk_ref, bgk_ref, gn_ref, s0_ref, o_ref, sout_ref):
    r = r_ref[...]
    for h in range(4):
        gk = jnp.sum(wgk_ref[h] * r, axis=-1, keepdims=True) + bgk_ref[h]
        logf = _log_sigmoid(gk) * (1.0 / GK_NORM)
        qq = q_ref[h] * ((LANES // 2) ** -0.5)
        s_new = jnp.exp(logf) * s0_ref[h] + k_ref[h] * v_ref[h]
        sout_ref[h] = s_new
        _dec_finish(s_new, qq, g_ref[h], gn_ref[...], o_ref, h)


def _dec_a(qa, fa, ia, ga, lb_gamma, gn_a, s0, l):
    nb = qa.shape[0]
    depth = lb_gamma.shape[0]
    colspec = pl.BlockSpec((None, 4, LANES, 1), lambda b: (b, 0, 0, 0))
    rowspec = pl.BlockSpec((None, 4, 1, LANES), lambda b: (b, 0, 0, 0))
    stspec = pl.BlockSpec((None, 4, LANES, LANES), lambda b: (b, 0, 0, 0))
    o, s = pl.pallas_call(
        functools.partial(_dec_a_kernel, l=l),
        grid=(nb,),
        in_specs=[colspec, colspec, rowspec, rowspec,
                  pl.BlockSpec((depth, 4, LANES, 1), lambda b: (0, 0, 0, 0)),
                  pl.BlockSpec((None, 1, LANES), lambda b: (l, 0, 0)),
                  stspec],
        out_specs=[rowspec, stspec],
        out_shape=[jax.ShapeDtypeStruct((nb, 4, 1, LANES), F32), jax.ShapeDtypeStruct(s0.shape, F32)],
        compiler_params=_cparams(("parallel",)),
        name="dec_hgrn",
    )(qa.reshape(nb, 4, LANES, 1), fa.reshape(nb, 4, LANES, 1), ia.reshape(nb, 4, 1, LANES),
      ga.reshape(nb, 4, 1, LANES), lb_gamma.reshape(depth, 4, LANES, 1), gn_a.reshape(depth, 1, LANES), s0)
    return o.reshape(nb, 1, 4 * LANES).astype(BF16), s


def _dec_b(qb, kb, vb, gb, rb, wgk_t, b_gk, gn_b, s0, l):
    nb = qb.shape[0]
    depth = gn_b.shape[0]
    rank = rb.shape[-1]
    colspec = pl.BlockSpec((None, 4, 64, 1), lambda b: (b, 0, 0, 0))
    rowspec = pl.BlockSpec((None, 4, 1, LANES), lambda b: (b, 0, 0, 0))
    stspec = pl.BlockSpec((None, 4, 64, LANES), lambda b: (b, 0, 0, 0))
    o, s = pl.pallas_call(
        _dec_b_kernel,
        grid=(nb,),
        in_specs=[colspec, colspec, rowspec, rowspec,
                  pl.BlockSpec((None, 1, rank), lambda b: (b, 0, 0)),
                  pl.BlockSpec((None, 4, 64, rank), lambda b: (l, 0, 0, 0)),
                  pl.BlockSpec((None, 4, 64, 1), lambda b: (l, 0, 0, 0)),
                  pl.BlockSpec((None, 1, LANES), lambda b: (l, 0, 0)),
                  stspec],
        out_specs=[rowspec, stspec],
        out_shape=[jax.ShapeDtypeStruct((nb, 4, 1, LANES), F32), jax.ShapeDtypeStruct(s0.shape, F32)],
        compiler_params=_cparams(("parallel",)),
        name="dec_gla",
    )(qb.reshape(nb, 4, 64, 1), kb.reshape(nb, 4, 64, 1), vb.reshape(nb, 4, 1, LANES),
      gb.reshape(nb, 4, 1, LANES), rb.reshape(nb, 1, rank), wgk_t, b_gk.reshape(depth, 4, 64, 1),
      gn_b.reshape(depth, 1, LANES), s0)
    return o.reshape(nb, 1, 4 * LANES).astype(BF16), s


def _suffix_matrix(n, rows_first):
    a = lax.broadcasted_iota(jnp.int32, (n, n), 0)
    b = lax.broadcasted_iota(jnp.int32, (n, n), 1)
    if rows_first:
        m = jnp.where(a >= b, 1.0, 0.0).astype(BF16)
        return jnp.concatenate([m, m], axis=0)
    m = jnp.where(b >= a, 1.0, 0.0).astype(BF16)
    return jnp.concatenate([m, m], axis=1)


def _sb_prompt_kernel(bias_ref, qa_ref, qb_ref, k_ref, v_ref, oa_ref, ob_ref,
                      kbf_ref, vbf_ref, uu_ref, z_ref, sp_ref, suf_ref, *, l, tq, nq, scale, unroll):
    h = pl.program_id(1)
    i = pl.program_id(2)
    n_off = nq - 1

    @pl.when(i == 0)
    def _():
        kbf_ref[...] = k_ref[...].astype(BF16)
        vbf_ref[...] = v_ref[...].astype(BF16)
        uu_ref[...] = _suffix_matrix(tq, rows_first=True)[0:tq]

    bias = bias_ref[l, h]
    qa = qa_ref[...].astype(BF16)
    qb = qb_ref[...].astype(BF16)
    uu = uu_ref[...]
    ti = lax.broadcasted_iota(jnp.int32, (tq, tq), 0)
    si = lax.broadcasted_iota(jnp.int32, (tq, tq), 1)
    causal = si < ti

    def off_kb(s):
        return jnp.where(s < i, i - 1 - s, nq - 2 - s)

    def logits(slot, q, kb, masked):
        start = pl.multiple_of(kb * tq, tq)
        z = _dot_nt(q, kbf_ref[pl.ds(start, tq), :]) * scale + bias
        sp = jnp.maximum(z, 0.0) + jnp.log(1.0 + jnp.exp(-jnp.abs(z)))
        if masked:
            sp = jnp.where(causal, sp, 0.0)
        z_ref[slot] = z
        sp_ref[slot] = sp.astype(BF16)

    def suffix(slot):
        suf_ref[slot] = _dot(sp_ref[slot], uu)

    def weigh(slot, kb, run, acc, masked):
        start = pl.multiple_of(kb * tq, tq)
        suf = suf_ref[slot]
        w = jnp.exp(z_ref[slot] - suf - run)
        if masked:
            w = jnp.where(causal, w, 0.0)
        acc = acc + _dot(w.astype(BF16), vbf_ref[pl.ds(start, tq), :])
        return run + suf[:, 0:1], acc

    def sweep(fn):
        def body(j, c):
            for u in range(unroll):
                fn(unroll * j + u)
            return c
        lax.fori_loop(0, n_off // unroll, body, 0)

    logits(0, qa, i, True)
    logits(1, qb, nq - 1 - i, True)
    sweep(lambda s: logits(2 + s, jnp.where(s < i, qa, qb), off_kb(s), False))
    suffix(0)
    suffix(1)
    sweep(lambda s: suffix(2 + s))

    run0 = jnp.zeros((tq, 1), F32)
    acc0 = jnp.zeros((tq, LANES), F32)
    run_a, acc_a = weigh(0, i, run0, acc0, True)
    run_b, acc_b = weigh(1, nq - 1 - i, run0, acc0, True)

    def body3(j, c):
        run, acc, out_a = c
        for u in range(unroll):
            s = unroll * j + u
            at_b = s == i
            out_a = jnp.where(at_b, acc, out_a)
            run = jnp.where(at_b, run_b, run)
            acc = jnp.where(at_b, acc_b, acc)
            run, acc = weigh(2 + s, off_kb(s), run, acc, False)
        return run, acc, out_a

    _, acc, out_a = lax.fori_loop(0, n_off // unroll, body3, (run_a, acc_a, acc_a))
    oa_ref[...] = out_a.astype(oa_ref.dtype)
    ob_ref[...] = acc.astype(ob_ref.dtype)


def _sb_prompt(y, b_sb, l, col_q, n_heads, tq):
    b, L, _ = y.shape
    cq = col_q // LANES
    nq = L // tq
    n_off = nq - 1
    assert nq % 2 == 0
    unroll = max(u for u in (1, 2, 3, 4, 5) if n_off % u == 0)
    half = nq // 2
    n_slots = nq + 1
    o_lo, o_hi = pl.pallas_call(
        functools.partial(_sb_prompt_kernel, l=l, tq=tq, nq=nq, scale=LANES ** -0.5, unroll=unroll),
        grid=(b, n_heads, half),
        in_specs=[
            pl.BlockSpec(memory_space=pltpu.SMEM),
            pl.BlockSpec((None, tq, LANES), lambda bi, h, i: (bi, i, cq + h)),
            pl.BlockSpec((None, tq, LANES), lambda bi, h, i: (bi, nq - 1 - i, cq + h)),
            pl.BlockSpec((None, L, LANES), lambda bi, h, i: (bi, 0, cq + n_heads + h)),
            pl.BlockSpec((None, L, LANES), lambda bi, h, i: (bi, 0, cq + 2 * n_heads + h)),
        ],
        out_specs=[pl.BlockSpec((None, tq, LANES), lambda bi, h, i: (bi, i, h)),
                   pl.BlockSpec((None, tq, LANES), lambda bi, h, i: (bi, half - 1 - i, h))],
        out_shape=[jax.ShapeDtypeStruct((b, L // 2, n_heads * LANES), BF16)] * 2,
        scratch_shapes=[pltpu.VMEM((L, LANES), BF16), pltpu.VMEM((L, LANES), BF16),
                        pltpu.VMEM((tq, tq), BF16),
                        pltpu.VMEM((n_slots, tq, tq), F32), pltpu.VMEM((n_slots, tq, tq), BF16),
                        pltpu.VMEM((n_slots, tq, tq), F32)],
        compiler_params=_cparams(("parallel", "parallel", "arbitrary")),
        name="sb_prompt",
    )(b_sb, y, y, y, y)
    return jnp.concatenate([o_lo, o_hi], axis=1)


def _sb_decode_kernel(pt_ref, bias_ref, q_ref, *refs, page, npg, scale):
    k_refs, v_refs = refs[:npg], refs[npg:2 * npg]
    o_ref, run_ref, acc_ref = refs[2 * npg:]
    p = pl.program_id(1)

    @pl.when(p == 0)
    def _():
        run_ref[...] = jnp.zeros_like(run_ref)
        acc_ref[...] = jnp.zeros_like(acc_ref)

    q = q_ref[...]
    bias = bias_ref[...]
    nh = q.shape[0]
    tok = lax.broadcasted_iota(jnp.int32, (page, nh, LANES), 0)
    lane = lax.broadcasted_iota(jnp.int32, (page, nh, LANES), 2)
    own_lane = tok == lane
    ones = jnp.ones((LANES, LANES), BF16)
    uu = _suffix_matrix(page, rows_first=True)
    run = run_ref[...]
    acc = acc_ref[...]
    for i in range(npg):
        prod = (k_refs[i][...] * q[None]).reshape(page * nh, LANES).astype(BF16)
        zrep = _dot(prod, ones).reshape(page, nh, LANES)
        z = jnp.sum(jnp.where(own_lane, zrep, 0.0), axis=0) * scale + bias
        lk = -(jnp.maximum(z, 0.0) + jnp.log(1.0 + jnp.exp(-jnp.abs(z))))
        hi = lk.astype(BF16)
        lo = (lk - hi.astype(F32)).astype(BF16)
        suf = _dot(jnp.concatenate([hi, lo], axis=1), uu)
        w = jnp.exp(z + suf + run)
        wsel = jnp.where(own_lane, w[None], 0.0).reshape(page * nh, LANES).astype(BF16)
        wrep = _dot(wsel, ones).reshape(page, nh, LANES)
        acc = acc + jnp.sum(wrep * v_refs[i][...], axis=0)
        run = run + suf[:, 0:1]
    run_ref[...] = run
    acc_ref[...] = acc

    @pl.when(p == pl.num_programs(1) - 1)
    def _():
        o_ref[...] = acc


def _sb_decode(q, cache_k, cache_v, page_table, b_sb, l):
    nb = q.shape[0]
    _, _, page, nh, dh = cache_k.shape
    n_pages = page_table.shape[1]
    npg = math.gcd(DEC_PAGES, n_pages)
    assert nh == SUBLANES and dh == LANES and page == LANES
    bias = jnp.broadcast_to(b_sb[l][:, None], (nh, LANES))

    def page_spec(i):
        return pl.BlockSpec((None, None, page, nh, dh),
                            lambda b, p, pt: (l, pt[b, n_pages - 1 - (p * npg + i)], 0, 0, 0))

    grid_spec = pltpu.PrefetchScalarGridSpec(
        num_scalar_prefetch=1,
        grid=(nb, n_pages // npg),
        in_specs=[pl.BlockSpec((nh, LANES), lambda b, p, pt: (0, 0)),
                  pl.BlockSpec((None, nh, dh), lambda b, p, pt: (b, 0, 0))]
                 + [page_spec(i) for i in range(npg)] * 2,
        out_specs=pl.BlockSpec((None, nh, dh), lambda b, p, pt: (b, 0, 0)),
        scratch_shapes=[pltpu.VMEM((nh, LANES), F32), pltpu.VMEM((nh, dh), F32)],
    )
    o = pl.pallas_call(
        functools.partial(_sb_decode_kernel, page=page, npg=npg, scale=dh ** -0.5),
        grid_spec=grid_spec,
        out_shape=jax.ShapeDtypeStruct((nb, nh, dh), F32),
        compiler_params=_cparams(("parallel", "arbitrary")),
        name="sb_decode",
    )(page_table, bias, q.reshape(nb, nh, dh), *([cache_k] * npg), *([cache_v] * npg))
    return o.reshape(nb, 1, nh * dh).astype(BF16)


def _trunk(path, x, mod_all, wts, state_a, state_b, cache, dims):
    d_model = x.shape[-1]
    depth = wts["w_o"].shape[0]
    c1, c2 = dims["col_rec"], dims["col_att"]
    is_prompt = cache is None
    tm = path.tm
    ks, vs, sas, sbs = [], [], [], []
    for l in range(depth):
        h = _norm_mod(path, x, wts["g_mix"], mod_all, l, 1, 0)
        y1 = _mm_nt(path, h, wts["w_in_rec"], l, c1["end"] // 2, name="in_proj_rec")
        y2 = _mm_nt(path, h, wts["w_in_att"], l, 1024, name="in_proj_att")
        r = _mm_nt(path, h, wts["w_in_rb"], l, LANES, name="in_proj_r")
        bv, lv, _ = y1.shape
        k_new = y2[:, :, c2["kc"]:c2["vc"]]
        v_new = y2[:, :, c2["vc"]:c2["ma"]]
        if is_prompt:
            o_a, s_a = _rec_a(y1, wts["lb_gamma"], wts["gn_a"], state_a[l], l, dims["chunk"])
            o_b, s_b = _rec_b(y1, r, wts["w_gk_p"], wts["b_gk"], wts["gn_b"], state_b[l], l, dims["chunk"], c1["qb"])
            o_c = _sb_prompt(y2, wts["b_sb"], l, c2["qc"], dims["h_c"], dims["tq"])
        else:
            f1, f2 = y1.reshape(lv, -1), y2.reshape(lv, -1)
            s1 = lambda a, b_: f1[:, c1[a]:c1[b_]]
            o_a, s_a = _dec_a(s1("qa", "fa"), s1("fa", "ia"), s1("ia", "ga"), s1("ga", "qb"),
                              wts["lb_gamma"], wts["gn_a"], state_a[l], l)
            o_b, s_b = _dec_b(s1("qb", "kb"), s1("kb", "vb"), s1("vb", "gb"), s1("gb", "end"),
                              r.reshape(lv, -1)[:, :dims["gk_rank"]], wts["w_gk_t"], wts["b_gk"], wts["gn_b"],
                              state_b[l], l)
            o_c = _sb_decode(f2[:, c2["qc"]:c2["kc"]], cache[0], cache[1], cache[2], wts["b_sb"], l)
            o_a, o_b, o_c = (o.reshape(bv, lv, -1) for o in (o_a, o_b, o_c))
        merged = _merge(path, o_a, o_b, o_c, y2, c2["ma"], wts["w_br"], l, 512)
        x = _mm_resid(path, merged, wts["w_o"], x, mod_all, l, 2, 1024, name="out_proj")
        h2 = _norm_mod(path, x, wts["g_ffn"], mod_all, l, 4, 3)
        act = _swiglu(path, h2, wts["w_gu"], l, 512)
        x = _mm_resid(path, act, wts["w_down"], x, mod_all, l, 5, 512, name="down_proj")
        ks.append(k_new)
        vs.append(v_new)
        sas.append(s_a)
        sbs.append(s_b)
    xf = _final_norm(x, wts["g_final"], tm)
    return xf, jnp.stack(ks), jnp.stack(vs), jnp.stack(sas), jnp.stack(sbs)


def kernel(x_prompt, x_sample, cache_k, cache_v, state_hgrn, state_gla, page_table, c_prompt, c_sample, w_ada, b_ada, g_mix, w_in, b_sb, lb_gamma, gn_a, w_gk, b_gk, gn_b, w_br, w_o, g_ffn, w_gu, w_down, g_final):
    bp, seq, d_model = x_prompt.shape
    nb = x_sample.shape[0]
    depth, _, _, h_c, dh_c = cache_k.shape[0], None, None, cache_k.shape[3], cache_k.shape[4]
    h_a, dk_a, dv_a = state_hgrn.shape[2:]
    h_b, dk_b, dv_b = state_gla.shape[2:]
    gk_rank = w_gk.shape[1]
    w_a, w_b, w_c = h_a * dv_a, h_b * dv_b, h_c * dh_c

    def offsets(widths):
        col, off = {}, 0
        for name, wd in widths:
            col[name] = off
            off += wd
        col["end"] = off
        return col

    col_rec = offsets([("qa", h_a * dk_a), ("fa", h_a * dk_a), ("ia", w_a), ("ga", w_a),
                       ("qb", h_b * dk_b), ("kb", h_b * dk_b), ("vb", w_b), ("gb", w_b)])
    col_att = offsets([("qc", w_c), ("kc", w_c), ("vc", w_c), ("ma", d_model), ("mb", d_model), ("mc", d_model)])
    src_rb = col_rec["end"]
    w_in_t = jnp.swapaxes(w_in, 1, 2)
    w_in_rb = jnp.concatenate([w_in_t[:, src_rb:src_rb + gk_rank],
                               jnp.zeros((depth, LANES - gk_rank, d_model), w_in.dtype)], axis=1).astype(BF16)
    w_gk_p = jnp.concatenate([w_gk, jnp.zeros((depth, LANES - gk_rank, w_gk.shape[-1]), w_gk.dtype)],
                             axis=1).astype(BF16)
    wts = {
        "w_in_rec": w_in_t[:, :src_rb].astype(BF16), "w_in_att": w_in_t[:, src_rb + gk_rank:].astype(BF16),
        "w_in_rb": w_in_rb,
        "w_br": w_br.astype(BF16), "w_o": w_o.astype(BF16), "w_gu": w_gu.astype(BF16),
        "w_down": w_down.astype(BF16), "w_gk_p": w_gk_p,
        "w_gk_t": jnp.swapaxes(w_gk, 1, 2).reshape(depth, h_b, dk_b, gk_rank),
        "g_mix": g_mix, "g_ffn": g_ffn, "g_final": g_final, "b_sb": b_sb, "lb_gamma": lb_gamma,
        "gn_a": gn_a, "gn_b": gn_b, "b_gk": b_gk,
    }
    dims = {"col_rec": col_rec, "col_att": col_att, "chunk": 256, "tq": 256, "h_c": h_c, "gk_rank": gk_rank}

    c_all = jnp.concatenate([c_sample, c_prompt, jnp.zeros((16 - nb - bp, d_model), F32)], axis=0)
    mod_all = _ada_mod(c_all, w_ada, b_ada)

    prompt = _Path(bp, seq, min(1024, seq), per_row_mod=False, mod_row0=nb)
    sample = _Path(1, nb, nb, per_row_mod=True, mod_row0=0)

    zeros_a = jnp.zeros((depth, bp) + state_hgrn.shape[2:], state_hgrn.dtype)
    zeros_b = jnp.zeros((depth, bp) + state_gla.shape[2:], state_gla.dtype)
    y_p, k_p, v_p, sa_p, sb_p = _trunk(prompt, x_prompt, mod_all, wts, zeros_a, zeros_b, None, dims)

    cache = (cache_k, cache_v, page_table)
    y_s, k_s, v_s, sa_s, sb_s = _trunk(sample, x_sample.reshape(1, nb, d_model), mod_all, wts,
                                       state_hgrn, state_gla, cache, dims)

    return (y_p, y_s.reshape(nb, 1, d_model),
            k_p.reshape(depth, bp, seq, h_c, dh_c), v_p.reshape(depth, bp, seq, h_c, dh_c),
            k_s.reshape(depth, nb, 1, h_c, dh_c), v_s.reshape(depth, nb, 1, h_c, dh_c),
            sa_p, sa_s, sb_p, sb_s)
```

```python
import functools
import math

import jax
import jax.numpy as jnp
from jax import lax
from jax.experimental import pallas as pl
from jax.experimental.pallas import tpu as pltpu

F32 = jnp.float32
BF16 = jnp.bfloat16

EPS = 1e-6
GK_NORM = 16.0
N_MOD = 6
LANES = 128
SUBLANES = 8
VMEM_LIMIT = 56 * 1024 * 1024
DEC_PAGES = 16


def _cparams(sem):
    return pltpu.CompilerParams(dimension_semantics=sem, vmem_limit_bytes=VMEM_LIMIT)


def _silu(x):
    return x * jax.nn.sigmoid(x)


def _log_sigmoid(x):
    return jnp.minimum(x, 0.0) - jnp.log1p(jnp.exp(-jnp.abs(x)))


def _logaddexp(a, b):
    m = jnp.maximum(a, b)
    return m + jnp.log1p(jnp.exp(-jnp.abs(a - b)))


def _dot(a, b):
    return jnp.dot(a, b, preferred_element_type=F32)


def _dot_nt(a, b):
    return lax.dot_general(a, b, (((1,), (1,)), ((), ())), preferred_element_type=F32)


def _dot_tn(a, b):
    return lax.dot_general(a, b, (((0,), (0,)), ((), ())), preferred_element_type=F32)


class _Path:
    def __init__(self, bv, lv, tm, per_row_mod, mod_row0):
        self.bv, self.lv, self.tm = bv, lv, tm
        self.tm_big = 2 * tm if lv % (2 * tm) == 0 else tm
        self.per_row_mod = per_row_mod
        self.mod_row0 = mod_row0

    def grid_rows(self):
        return (self.bv, self.lv // self.tm)

    def mod_operand(self, mod_all):
        if self.per_row_mod:
            return mod_all
        d, r, w = mod_all.shape
        return mod_all.reshape(d, r, 1, w)

    def mod_spec(self, l, k, d_model, tn=None, with_j=False):
        tn = d_model if tn is None else tn
        per = d_model // tn
        if self.per_row_mod:
            if with_j:
                return pl.BlockSpec((None, self.tm, tn), lambda b, i, j: (l, i, k * per + j))
            return pl.BlockSpec((None, self.tm, tn), lambda b, i: (l, i, k * per))
        r0 = self.mod_row0
        if with_j:
            return pl.BlockSpec((None, None, 1, tn), lambda b, i, j: (l, r0 + b, 0, k * per + j))
        return pl.BlockSpec((None, None, 1, tn), lambda b, i: (l, r0 + b, 0, k * per))


def _ada_kernel(c_ref, w_ref, b_ref, o_ref):
    a = _silu(c_ref[...]).astype(BF16)
    o_ref[...] = _dot(a, w_ref[...].astype(BF16)) + b_ref[...]


def _ada_mod(c_all, w_ada, b_ada, tn=1024):
    depth, d, n = w_ada.shape
    rows = c_all.shape[0]
    return pl.pallas_call(
        _ada_kernel,
        grid=(depth, n // tn),
        in_specs=[
            pl.BlockSpec((rows, d), lambda l, j: (0, 0)),
            pl.BlockSpec((None, d, tn), lambda l, j: (l, 0, j)),
            pl.BlockSpec((None, 1, tn), lambda l, j: (l, 0, j)),
        ],
        out_specs=pl.BlockSpec((None, rows, tn), lambda l, j: (l, 0, j)),
        out_shape=jax.ShapeDtypeStruct((depth, rows, n), F32),
        compiler_params=_cparams(("parallel", "parallel")),
        name="ada_mod",
    )(c_all, w_ada, b_ada.reshape(depth, 1, n))


def _norm_mod_kernel(x_ref, g_ref, sc_ref, sh_ref, o_ref):
    x = x_ref[...]
    y = x * lax.rsqrt(jnp.mean(x * x, axis=-1, keepdims=True) + EPS) * g_ref[...]
    o_ref[...] = (y * (1.0 + sc_ref[...]) + sh_ref[...]).astype(o_ref.dtype)


def _norm_kernel(x_ref, g_ref, o_ref):
    x = x_ref[...]
    y = x * lax.rsqrt(jnp.mean(x * x, axis=-1, keepdims=True) + EPS) * g_ref[...]
    o_ref[...] = y.astype(o_ref.dtype)


def _norm_mod(path, x, g, mod_all, l, k_sc, k_sh, tm=None):
    bv, lv, d = x.shape
    tm = path.tm if tm is None else tm
    p = _Path(bv, lv, tm, path.per_row_mod, path.mod_row0)
    depth = g.shape[0]
    mod = p.mod_operand(mod_all)
    return pl.pallas_call(
        _norm_mod_kernel,
        grid=p.grid_rows(),
        in_specs=[
            pl.BlockSpec((None, tm, d), lambda b, i: (b, i, 0)),
            pl.BlockSpec((None, 1, d), lambda b, i: (l, 0, 0)),
            p.mod_spec(l, k_sc, d),
            p.mod_spec(l, k_sh, d),
        ],
        out_specs=pl.BlockSpec((None, tm, d), lambda b, i: (b, i, 0)),
        out_shape=jax.ShapeDtypeStruct((bv, lv, d), BF16),
        compiler_params=_cparams(("parallel", "parallel")),
        name="norm_mod",
    )(x, g.reshape(depth, 1, d), mod, mod)


def _final_norm(x, g, tm):
    bv, lv, d = x.shape
    return pl.pallas_call(
        _norm_kernel,
        grid=(bv, lv // tm),
        in_specs=[
            pl.BlockSpec((None, tm, d), lambda b, i: (b, i, 0)),
            pl.BlockSpec((1, d), lambda b, i: (0, 0)),
        ],
        out_specs=pl.BlockSpec((None, tm, d), lambda b, i: (b, i, 0)),
        out_shape=jax.ShapeDtypeStruct((bv, lv, d), F32),
        compiler_params=_cparams(("parallel", "parallel")),
        name="final_norm",
    )(x, g.reshape(1, d))


def _mm_nt_kernel(a_ref, w_ref, o_ref):
    o_ref[...] = _dot_nt(a_ref[...], w_ref[...]).astype(o_ref.dtype)


def _mm_nt(path, a, w_t, l, tn, tm=None, out_dtype=F32, name="mm"):
    bv, lv, k = a.shape
    ncols = w_t.shape[1]
    tm = path.tm if tm is None else tm
    assert ncols % tn == 0 and lv % tm == 0
    return pl.pallas_call(
        _mm_nt_kernel,
        grid=(bv, lv // tm, ncols // tn),
        in_specs=[
            pl.BlockSpec((None, tm, k), lambda b, i, j: (b, i, 0)),
            pl.BlockSpec((None, tn, k), lambda b, i, j: (l, j, 0)),
        ],
        out_specs=pl.BlockSpec((None, tm, tn), lambda b, i, j: (b, i, j)),
        out_shape=jax.ShapeDtypeStruct((bv, lv, ncols), out_dtype),
        compiler_params=_cparams(("parallel", "parallel", "arbitrary")),
        name=name,
    )(a, w_t)


def _in_proj_att_kernel(*refs, aliased):
    if aliased:
        a_ref, w_ref, _, _, y_ref, k_ref, v_ref = refs
    else:
        a_ref, w_ref, y_ref, k_ref, v_ref = refs
    j = pl.program_id(2)

    @pl.when(j == 1)
    def _():
        k_ref[...] = _dot_nt(a_ref[...], w_ref[...])

    @pl.when(j == 2)
    def _():
        v_ref[...] = _dot_nt(a_ref[...], w_ref[...])

    @pl.when(jnp.logical_or(j == 0, j >= 3))
    def _():
        y_ref[...] = _dot_nt(a_ref[...], w_ref[...])


def _in_proj_att(path, a, w_t, l, kbuf, vbuf, tm=None):
    bv, lv, k = a.shape
    depth, ncols, _ = w_t.shape
    tn = ncols // 9
    tm = path.tm if tm is None else tm
    aliased = kbuf is not None
    kv_shape = jax.ShapeDtypeStruct((depth, bv, lv, tn), F32)
    kv_spec = pl.BlockSpec((None, None, tm, tn), lambda b, i, j: (l, b, i, 0))
    any_spec = pl.BlockSpec(memory_space=pl.ANY)
    return pl.pallas_call(
        functools.partial(_in_proj_att_kernel, aliased=aliased),
        grid=(bv, lv // tm, 9),
        in_specs=[
            pl.BlockSpec((None, tm, k), lambda b, i, j: (b, i, 0)),
            pl.BlockSpec((None, tn, k), lambda b, i, j: (l, j, 0)),
        ] + ([any_spec, any_spec] if aliased else []),
        out_specs=[pl.BlockSpec((None, tm, tn), lambda b, i, j: (b, i, jnp.maximum(j - 2, 0))), kv_spec, kv_spec],
        out_shape=[jax.ShapeDtypeStruct((bv, lv, 7 * tn), F32), kv_shape, kv_shape],
        input_output_aliases={2: 1, 3: 2} if aliased else {},
        compiler_params=_cparams(("parallel", "parallel", "arbitrary")),
        name="in_proj_att",
    )(a, w_t, *((kbuf, vbuf) if aliased else ()))


def _mm_resid_kernel(a_ref, w_ref, x_ref, gt_ref, o_ref):
    o_ref[...] = x_ref[...] + gt_ref[...] * _dot(a_ref[...], w_ref[...])


def _mm_resid(path, a, w, x, mod_all, l, k_gt, tn, tm=None, name="mm_resid"):
    bv, lv, k = a.shape
    d = w.shape[-1]
    tm = path.tm if tm is None else tm
    p = _Path(bv, lv, tm, path.per_row_mod, path.mod_row0)
    return pl.pallas_call(
        _mm_resid_kernel,
        grid=(bv, lv // tm, d // tn),
        in_specs=[
            pl.BlockSpec((None, tm, k), lambda b, i, j: (b, i, 0)),
            pl.BlockSpec((None, k, tn), lambda b, i, j: (l, 0, j)),
            pl.BlockSpec((None, tm, tn), lambda b, i, j: (b, i, j)),
            p.mod_spec(l, k_gt, d, tn=tn, with_j=True),
        ],
        out_specs=pl.BlockSpec((None, tm, tn), lambda b, i, j: (b, i, j)),
        out_shape=jax.ShapeDtypeStruct((bv, lv, d), F32),
        compiler_params=_cparams(("parallel", "parallel", "arbitrary")),
        name=name,
    )(a, w, x, p.mod_operand(mod_all))


def _swiglu_kernel(a_ref, wg_ref, wu_ref, o_ref):
    a = a_ref[...]
    o_ref[...] = (_silu(_dot(a, wg_ref[...])) * _dot(a, wu_ref[...])).astype(o_ref.dtype)


def _swiglu(path, a, w_gu, l, tn, tm=None):
    bv, lv, k = a.shape
    d_ff = w_gu.shape[-1] // 2
    tm = path.tm if tm is None else tm
    nj = d_ff // tn
    assert d_ff % tn == 0
    return pl.pallas_call(
        _swiglu_kernel,
        grid=(bv, lv // tm, nj),
        in_specs=[
            pl.BlockSpec((None, tm, k), lambda b, i, j: (b, i, 0)),
            pl.BlockSpec((None, k, tn), lambda b, i, j: (l, 0, j)),
            pl.BlockSpec((None, k, tn), lambda b, i, j: (l, 0, nj + j)),
        ],
        out_specs=pl.BlockSpec((None, tm, tn), lambda b, i, j: (b, i, j)),
        out_shape=jax.ShapeDtypeStruct((bv, lv, d_ff), BF16),
        compiler_params=_cparams(("parallel", "parallel", "arbitrary")),
        name="swiglu",
    )(a, w_gu, w_gu)


def _merge_kernel(oa_ref, ob_ref, oc_ref, ma_ref, mb_ref, mc_ref, wa_ref, wb_ref, wc_ref, o_ref):
    m = jax.nn.sigmoid(ma_ref[...]) * _dot(oa_ref[...], wa_ref[...])
    m = m + jax.nn.sigmoid(mb_ref[...]) * _dot(ob_ref[...], wb_ref[...])
    m = m + jax.nn.sigmoid(mc_ref[...]) * _dot(oc_ref[...], wc_ref[...])
    o_ref[...] = m.astype(o_ref.dtype)


def _merge(path, o_a, o_b, o_c, y, gate_col0, w_br, l, tn, tm=None):
    bv, lv, wa = o_a.shape
    wb, wc = o_b.shape[-1], o_c.shape[-1]
    d = w_br.shape[-1]
    tm = path.tm if tm is None else tm
    g0 = gate_col0 // tn
    per = d // tn
    assert gate_col0 % tn == 0 and wb == wa and wc == 2 * wa
    return pl.pallas_call(
        _merge_kernel,
        grid=(bv, lv // tm, d // tn),
        in_specs=[
            pl.BlockSpec((None, tm, wa), lambda b, i, j: (b, i, 0)),
            pl.BlockSpec((None, tm, wb), lambda b, i, j: (b, i, 0)),
            pl.BlockSpec((None, tm, wc), lambda b, i, j: (b, i, 0)),
            pl.BlockSpec((None, tm, tn), lambda b, i, j: (b, i, g0 + j)),
            pl.BlockSpec((None, tm, tn), lambda b, i, j: (b, i, g0 + per + j)),
            pl.BlockSpec((None, tm, tn), lambda b, i, j: (b, i, g0 + 2 * per + j)),
            pl.BlockSpec((None, wa, tn), lambda b, i, j: (l, 0, j)),
            pl.BlockSpec((None, wb, tn), lambda b, i, j: (l, 1, j)),
            pl.BlockSpec((None, wc, tn), lambda b, i, j: (l, 1, j)),
        ],
        out_specs=pl.BlockSpec((None, tm, tn), lambda b, i, j: (b, i, j)),
        out_shape=jax.ShapeDtypeStruct((bv, lv, d), BF16),
        compiler_params=_cparams(("parallel", "parallel", "arbitrary")),
        name="merge",
    )(o_a, o_b, o_c, y, y, y, w_br, w_br, w_br)


def _lb_from_gamma(gam, l):
    depth = len(gam)
    m = gam[0]
    for i in range(1, depth):
        m = jnp.maximum(m, gam[i])
    e = [jnp.exp(gam[i] - m) for i in range(depth)]
    tot = e[0]
    for i in range(1, depth):
        tot = tot + e[i]
    sm = [ei / tot for ei in e]
    cs = [sm[0]]
    for i in range(1, depth):
        cs.append(cs[-1] + sm[i])
    return cs[l] - cs[0]


def _rec_core(qq, kk, logf, v_ref_tile, g_tile, gn, st_ref, o_ref, *, chunk, n_groups, hpg):
    c = chunk
    wq = n_groups * LANES
    dk = LANES // hpg
    n_levels = int(math.log2(c))
    row = lax.broadcasted_iota(jnp.int32, (c, wq), 0)
    ti = lax.broadcasted_iota(jnp.int32, (c, c), 0)
    si = lax.broadcasted_iota(jnp.int32, (c, c), 1)
    x = jnp.bitwise_xor(ti, si)
    lvl = jnp.zeros((c, c), jnp.int32)
    for lev in range(1, n_levels + 1):
        lvl = lvl + jnp.where(x >= (1 << (lev - 1)), 1, 0)
    lvl = jnp.where(ti > si, lvl, -1)
    lane = lax.broadcasted_iota(jnp.int32, (c, LANES), 1)

    def head_mask(a, sub):
        if hpg == 1:
            return a
        return jnp.where(lane < dk, a, 0.0) if sub == 0 else jnp.where(lane >= dk, a, 0.0)

    pre = logf
    tot = logf
    ql, kl = [], []
    for lev in range(1, n_levels + 1):
        half = 1 << (lev - 1)
        upper = (row & half) != 0
        e = jnp.exp(jnp.where(upper, pre, tot - pre))
        ql.append(qq * e)
        kl.append(kk * e)
        up = pltpu.roll(tot, half, 0)
        dn = pltpu.roll(tot, c - half, 0)
        pre = pre + jnp.where(upper, up, 0.0)
        tot = tot + jnp.where(upper, up, dn)
    q_in = qq * jnp.exp(pre)
    k_out = kk * jnp.exp(tot - pre)
    d_all = jnp.exp(tot[0:1, :])
    qk = qq * kk

    for g in range(n_groups):
        gs = slice(g * LANES, (g + 1) * LANES)
        st = st_ref[g]
        st_bf = st.astype(BF16)
        st_new = st * d_all[:, gs]
        for sub in range(hpg):
            h = g * hpg + sub
            hs = slice(h * LANES, (h + 1) * LANES)
            v_bf = v_ref_tile[:, hs].astype(BF16)
            att = jnp.zeros((c, c), F32)
            for lev in range(1, n_levels + 1):
                p = _dot_nt(head_mask(ql[lev - 1][:, gs], sub).astype(BF16), kl[lev - 1][:, gs].astype(BF16))
                att = jnp.where(lvl == lev, p, att)
            diag = jnp.sum(head_mask(qk[:, gs], sub), axis=-1, keepdims=True)
            att = jnp.where(x == 0, diag, att)
            o = _dot_nt(head_mask(q_in[:, gs], sub).astype(BF16), st_bf) + _dot(att.astype(BF16), v_bf)
            st_new = st_new + _dot_tn(v_bf, head_mask(k_out[:, gs], sub).astype(BF16))
            y = o * lax.rsqrt(jnp.mean(o * o, axis=-1, keepdims=True) + EPS) * gn
            o_ref[:, hs] = (y * _silu(g_tile[:, hs])).astype(o_ref.dtype)
        st_ref[g] = st_new


def _state_in(s0_ref, st_ref, n_groups, hpg):
    dk = LANES // hpg
    for g in range(n_groups):
        blk = jnp.concatenate([s0_ref[g * hpg + sub] for sub in range(hpg)], axis=0) if hpg > 1 else s0_ref[g]
        st_ref[g] = blk.T


def _state_out(st_ref, sout_ref, n_groups, hpg):
    dk = LANES // hpg
    for g in range(n_groups):
        t = st_ref[g].T
        for sub in range(hpg):
            sout_ref[g * hpg + sub] = t[sub * dk:(sub + 1) * dk, :]


def _rec_a_kernel(q_ref, f_ref, i_ref, g_ref, lbg_ref, gn_ref, s0_ref, o_ref, sout_ref, st_ref, *, l, chunk):
    ci = pl.program_id(1)

    @pl.when(ci == 0)
    def _():
        _state_in(s0_ref, st_ref, 4, 1)

    lb = _lb_from_gamma([lbg_ref[i:i + 1, :] for i in range(lbg_ref.shape[0])], l)
    f = f_ref[...]
    logf = _logaddexp(jnp.log(lb), jnp.log1p(-lb) + _log_sigmoid(f))
    kk = (1.0 - lb) * jax.nn.sigmoid(-f)
    qq = _silu(q_ref[...]) * (LANES ** -0.5)
    _rec_core(qq, kk, logf, i_ref, g_ref[...], gn_ref[...], st_ref, o_ref, chunk=chunk, n_groups=4, hpg=1)

    @pl.when(ci == pl.num_programs(1) - 1)
    def _():
        _state_out(st_ref, sout_ref, 4, 1)


def _rec_b_kernel(q_ref, k_ref, v_ref, g_ref, r_ref, wgk_ref, bgk_ref, gn_ref, s0_ref, o_ref, sout_ref, st_ref,
                  *, chunk):
    ci = pl.program_id(1)

    @pl.when(ci == 0)
    def _():
        _state_in(s0_ref, st_ref, 2, 2)

    gk = _dot(r_ref[...].astype(BF16), wgk_ref[...]) + bgk_ref[...]
    logf = _log_sigmoid(gk) * (1.0 / GK_NORM)
    qq = q_ref[...] * ((LANES // 2) ** -0.5)
    _rec_core(qq, k_ref[...], logf, v_ref, g_ref[...], gn_ref[...], st_ref, o_ref, chunk=chunk, n_groups=2, hpg=2)

    @pl.when(ci == pl.num_programs(1) - 1)
    def _():
        _state_out(st_ref, sout_ref, 2, 2)


def _rec_a(y, lb_gamma, gn_a, s0, l, chunk):
    b, L, _ = y.shape
    depth = lb_gamma.shape[0]
    w = 4 * LANES
    col = lambda k: pl.BlockSpec((None, chunk, w), lambda bi, ci: (bi, ci, k))
    return pl.pallas_call(
        functools.partial(_rec_a_kernel, l=l, chunk=chunk),
        grid=(b, L // chunk),
        in_specs=[
            col(0), col(1), col(2), col(3),
            pl.BlockSpec((depth, w), lambda bi, ci: (0, 0)),
            pl.BlockSpec((None, 1, LANES), lambda bi, ci: (l, 0, 0)),
            pl.BlockSpec((None, 4, LANES, LANES), lambda bi, ci: (bi, 0, 0, 0)),
        ],
        out_specs=[
            pl.BlockSpec((None, chunk, w), lambda bi, ci: (bi, ci, 0)),
            pl.BlockSpec((None, 4, LANES, LANES), lambda bi, ci: (bi, 0, 0, 0)),
        ],
        out_shape=[jax.ShapeDtypeStruct((b, L, w), BF16), jax.ShapeDtypeStruct(s0.shape, F32)],
        scratch_shapes=[pltpu.VMEM((4, LANES, LANES), F32)],
        compiler_params=_cparams(("parallel", "arbitrary")),
        name="rec_hgrn",
    )(y, y, y, y, lb_gamma, gn_a.reshape(depth, 1, LANES), s0)


def _rec_b(y, r, wgk_p, b_gk, gn_b, s0, l, chunk, col_b):
    b, L, _ = y.shape
    depth = gn_b.shape[0]
    c256 = col_b // 256
    c512 = (col_b + 512) // 512
    return pl.pallas_call(
        functools.partial(_rec_b_kernel, chunk=chunk),
        grid=(b, L // chunk),
        in_specs=[
            pl.BlockSpec((None, chunk, 256), lambda bi, ci: (bi, ci, c256)),
            pl.BlockSpec((None, chunk, 256), lambda bi, ci: (bi, ci, c256 + 1)),
            pl.BlockSpec((None, chunk, 512), lambda bi, ci: (bi, ci, c512)),
            pl.BlockSpec((None, chunk, 512), lambda bi, ci: (bi, ci, c512 + 1)),
            pl.BlockSpec((None, chunk, LANES), lambda bi, ci: (bi, ci, 0)),
            pl.BlockSpec((None, LANES, 256), lambda bi, ci: (l, 0, 0)),
            pl.BlockSpec((None, 1, 256), lambda bi, ci: (l, 0, 0)),
            pl.BlockSpec((None, 1, LANES), lambda bi, ci: (l, 0, 0)),
            pl.BlockSpec((None, 4, 64, LANES), lambda bi, ci: (bi, 0, 0, 0)),
        ],
        out_specs=[
            pl.BlockSpec((None, chunk, 512), lambda bi, ci: (bi, ci, 0)),
            pl.BlockSpec((None, 4, 64, LANES), lambda bi, ci: (bi, 0, 0, 0)),
        ],
        out_shape=[jax.ShapeDtypeStruct((b, L, 512), BF16), jax.ShapeDtypeStruct(s0.shape, F32)],
        scratch_shapes=[pltpu.VMEM((2, LANES, LANES), F32)],
        compiler_params=_cparams(("parallel", "arbitrary")),
        name="rec_gla",
    )(y, y, y, y, r, wgk_p, b_gk.reshape(depth, 1, 256), gn_b.reshape(depth, 1, LANES), s0)


def _dec_finish(s_new, q_col, g_row, gn, o_ref, h):
    o = jnp.sum(q_col * s_new, axis=0, keepdims=True)
    y = o * lax.rsqrt(jnp.mean(o * o, axis=-1, keepdims=True) + EPS) * gn
    o_ref[h] = y * _silu(g_row)


def _dec_a_kernel(q_ref, f_ref, i_ref, g_ref, lbg_ref, gn_ref, s0_ref, o_ref, sout_ref, *, l):
    lb_all = _lb_from_gamma([lbg_ref[i] for i in range(lbg_ref.shape[0])], l)
    for h in range(4):
        lb = lb_all[h]
        f = f_ref[h]
        logf = _logaddexp(jnp.log(lb), jnp.log1p(-lb) + _log_sigmoid(f))
        kk = (1.0 - lb) * jax.nn.sigmoid(-f)
        qq = _silu(q_ref[h]) * (LANES ** -0.5)
        s_new = jnp.exp(logf) * s0_ref[h] + kk * i_ref[h]
        sout_ref[h] = s_new
        _dec_finish(s_new, qq, g_ref[h], gn_ref[...], o_ref, h)


def _dec_b_kernel(q_ref, k_ref, v_ref, g_ref, r_ref, wgk_ref, bgk_ref, gn_ref, s0_ref, o_ref, sout_ref):
    r = r_ref[...]
    for h in range(4):
        gk = jnp.sum(wgk_ref[h] * r, axis=-1, keepdims=True) + bgk_ref[h]
        logf = _log_sigmoid(gk) * (1.0 / GK_NORM)
        qq = q_ref[h] * ((LANES // 2) ** -0.5)
        s_new = jnp.exp(logf) * s0_ref[h] + k_ref[h] * v_ref[h]
        sout_ref[h] = s_new
        _dec_finish(s_new, qq, g_ref[h], gn_ref[...], o_ref, h)


def _dec_a(qa, fa, ia, ga, lb_gamma, gn_a, s0, l):
    nb = qa.shape[0]
    depth = lb_gamma.shape[0]
    colspec = pl.BlockSpec((None, 4, LANES, 1), lambda b: (b, 0, 0, 0))
    rowspec = pl.BlockSpec((None, 4, 1, LANES), lambda b: (b, 0, 0, 0))
    stspec = pl.BlockSpec((None, 4, LANES, LANES), lambda b: (b, 0, 0, 0))
    o, s = pl.pallas_call(
        functools.partial(_dec_a_kernel, l=l),
        grid=(nb,),
        in_specs=[colspec, colspec, rowspec, rowspec,
                  pl.BlockSpec((depth, 4, LANES, 1), lambda b: (0, 0, 0, 0)),
                  pl.BlockSpec((None, 1, LANES), lambda b: (l, 0, 0)),
                  stspec],
        out_specs=[rowspec, stspec],
        out_shape=[jax.ShapeDtypeStruct((nb, 4, 1, LANES), F32), jax.ShapeDtypeStruct(s0.shape, F32)],
        compiler_params=_cparams(("parallel",)),
        name="dec_hgrn",
    )(qa.reshape(nb, 4, LANES, 1), fa.reshape(nb, 4, LANES, 1), ia.reshape(nb, 4, 1, LANES),
      ga.reshape(nb, 4, 1, LANES), lb_gamma.reshape(depth, 4, LANES, 1), gn_a.reshape(depth, 1, LANES), s0)
    return o.reshape(nb, 1, 4 * LANES).astype(BF16), s


def _dec_b(qb, kb, vb, gb, rb, wgk_t, b_gk, gn_b, s0, l):
    nb = qb.shape[0]
    depth = gn_b.shape[0]
    rank = rb.shape[-1]
    colspec = pl.BlockSpec((None, 4, 64, 1), lambda b: (b, 0, 0, 0))
    rowspec = pl.BlockSpec((None, 4, 1, LANES), lambda b: (b, 0, 0, 0))
    stspec = pl.BlockSpec((None, 4, 64, LANES), lambda b: (b, 0, 0, 0))
    o, s = pl.pallas_call(
        _dec_b_kernel,
        grid=(nb,),
        in_specs=[colspec, colspec, rowspec, rowspec,
                  pl.BlockSpec((None, 1, rank), lambda b: (b, 0, 0)),
                  pl.BlockSpec((None, 4, 64, rank), lambda b: (l, 0, 0, 0)),
                  pl.BlockSpec((None, 4, 64, 1), lambda b: (l, 0, 0, 0)),
                  pl.BlockSpec((None, 1, LANES), lambda b: (l, 0, 0)),
                  stspec],
        out_specs=[rowspec, stspec],
        out_shape=[jax.ShapeDtypeStruct((nb, 4, 1, LANES), F32), jax.ShapeDtypeStruct(s0.shape, F32)],
        compiler_params=_cparams(("parallel",)),
        name="dec_gla",
    )(qb.reshape(nb, 4, 64, 1), kb.reshape(nb, 4, 64, 1), vb.reshape(nb, 4, 1, LANES),
      gb.reshape(nb, 4, 1, LANES), rb.reshape(nb, 1, rank), wgk_t, b_gk.reshape(depth, 4, 64, 1),
      gn_b.reshape(depth, 1, LANES), s0)
    return o.reshape(nb, 1, 4 * LANES).astype(BF16), s


def _suffix_matrix(n, rows_first):
    a = lax.broadcasted_iota(jnp.int32, (n, n), 0)
    b = lax.broadcasted_iota(jnp.int32, (n, n), 1)
    if rows_first:
        m = jnp.where(a >= b, 1.0, 0.0).astype(BF16)
        return jnp.concatenate([m, m], axis=0)
    m = jnp.where(b >= a, 1.0, 0.0).astype(BF16)
    return jnp.concatenate([m, m], axis=1)


def _sb_prompt_kernel(bias_ref, qa_ref, qb_ref, k_ref, v_ref, oa_ref, ob_ref,
                      kbf_ref, vbf_ref, uu_ref, z_ref, sp_ref, suf_ref, *, l, tq, nq, scale, unroll):
    h = pl.program_id(1)
    i = pl.program_id(2)
    n_off = nq - 1

    @pl.when(i == 0)
    def _():
        kbf_ref[...] = k_ref[...].astype(BF16)
        vbf_ref[...] = v_ref[...].astype(BF16)
        uu_ref[...] = _suffix_matrix(tq, rows_first=True)[0:tq]

    bias = bias_ref[l, h]
    qa = qa_ref[...].astype(BF16)
    qb = qb_ref[...].astype(BF16)
    uu = uu_ref[...]
    ti = lax.broadcasted_iota(jnp.int32, (tq, tq), 0)
    si = lax.broadcasted_iota(jnp.int32, (tq, tq), 1)
    causal = si < ti

    def off_kb(s):
        return jnp.where(s < i, i - 1 - s, nq - 2 - s)

    def logits(slot, q, kb, masked):
        start = pl.multiple_of(kb * tq, tq)
        z = _dot_nt(q, kbf_ref[pl.ds(start, tq), :]) * scale + bias
        neg_abs = lax.bitcast_convert_type(
            lax.bitcast_convert_type(z, jnp.uint32) | jnp.uint32(0x80000000), F32)
        sp = jnp.maximum(z, 0.0) + jnp.log(1.0 + jnp.exp(neg_abs))
        if masked:
            sp = jnp.where(causal, sp, 0.0)
        z_ref[slot] = z
        sp_ref[slot] = sp.astype(BF16)

    def suffix(slot):
        suf_ref[slot] = _dot(sp_ref[slot], uu)

    def weigh(slot, kb, run, acc, diagonal):
        start = pl.multiple_of(kb * tq, tq)
        suf = suf_ref[slot]
        if diagonal:
            w = jnp.where(causal, jnp.exp(z_ref[slot] - suf), 0.0)
            return suf[:, 0:1], _dot(w.astype(BF16), vbf_ref[pl.ds(start, tq), :])
        w = jnp.exp(z_ref[slot] - suf - run)
        return run + suf[:, 0:1], acc + _dot(w.astype(BF16), vbf_ref[pl.ds(start, tq), :])

    def sweep(diagonals, fn):
        diagonals()
        for u in range(unroll):
            fn(u)

        def body(j, c):
            for u in range(unroll):
                fn(unroll * j + u)
            return c
        lax.fori_loop(1, n_off // unroll, body, 0)

    sweep(lambda: (logits(0, qa, i, True), logits(1, qb, nq - 1 - i, True)),
          lambda s: logits(2 + s, jnp.where(s < i, qa, qb), off_kb(s), False))
    sweep(lambda: (suffix(0), suffix(1)), lambda s: suffix(2 + s))

    run_a, acc_a = weigh(0, i, None, None, True)
    run_b, acc_b = weigh(1, nq - 1 - i, None, None, True)

    def chain(s, c):
        run, acc, out_a = c
        at_b = s == i
        out_a = jnp.where(at_b, acc, out_a)
        run = jnp.where(at_b, run_b, run)
        acc = jnp.where(at_b, acc_b, acc)
        run, acc = weigh(2 + s, off_kb(s), run, acc, False)
        return run, acc, out_a

    def body3(j, c):
        for u in range(unroll):
            c = chain(unroll * j + u, c)
        return c

    c = (run_a, acc_a, acc_a)
    for u in range(unroll):
        c = chain(u, c)
    _, acc, out_a = lax.fori_loop(1, n_off // unroll, body3, c)
    oa_ref[...] = out_a.astype(oa_ref.dtype)
    ob_ref[...] = acc.astype(ob_ref.dtype)


def _sb_prompt(y, kbuf, vbuf, b_sb, l, col_q, n_heads, tq):
    b, L, _ = y.shape
    cq = col_q // LANES
    nq = L // tq
    n_off = nq - 1
    assert nq % 2 == 0
    unroll = max(u for u in (1, 2, 3, 4, 5) if n_off % u == 0)
    half = nq // 2
    n_slots = nq + 1
    o_lo, o_hi = pl.pallas_call(
        functools.partial(_sb_prompt_kernel, l=l, tq=tq, nq=nq, scale=LANES ** -0.5, unroll=unroll),
        grid=(b, n_heads, half),
        in_specs=[
            pl.BlockSpec(memory_space=pltpu.SMEM),
            pl.BlockSpec((None, tq, LANES), lambda bi, h, i: (bi, i, cq + h)),
            pl.BlockSpec((None, tq, LANES), lambda bi, h, i: (bi, nq - 1 - i, cq + h)),
            pl.BlockSpec((None, None, L, LANES), lambda bi, h, i: (l, bi, 0, h)),
            pl.BlockSpec((None, None, L, LANES), lambda bi, h, i: (l, bi, 0, h)),
        ],
        out_specs=[pl.BlockSpec((None, tq, LANES), lambda bi, h, i: (bi, i, h)),
                   pl.BlockSpec((None, tq, LANES), lambda bi, h, i: (bi, half - 1 - i, h))],
        out_shape=[jax.ShapeDtypeStruct((b, L // 2, n_heads * LANES), BF16)] * 2,
        scratch_shapes=[pltpu.VMEM((L, LANES), BF16), pltpu.VMEM((L, LANES), BF16),
                        pltpu.VMEM((tq, tq), BF16),
                        pltpu.VMEM((n_slots, tq, tq), F32), pltpu.VMEM((n_slots, tq, tq), BF16),
                        pltpu.VMEM((n_slots, tq, tq), F32)],
        compiler_params=_cparams(("parallel", "parallel", "arbitrary")),
        name="sb_prompt",
    )(b_sb, y, y, kbuf, vbuf)
    return jnp.concatenate([o_lo, o_hi], axis=1)


def _sb_decode_kernel(pt_ref, bias_ref, q_ref, *refs, page, npg, scale):
    k_refs, v_refs = refs[:npg], refs[npg:2 * npg]
    o_ref, run_ref, acc_ref = refs[2 * npg:]
    p = pl.program_id(1)

    @pl.when(p == 0)
    def _():
        run_ref[...] = jnp.zeros_like(run_ref)
        acc_ref[...] = jnp.zeros_like(acc_ref)

    q = q_ref[...]
    bias = bias_ref[...]
    nh = q.shape[0]
    tok = lax.broadcasted_iota(jnp.int32, (page, nh, LANES), 0)
    lane = lax.broadcasted_iota(jnp.int32, (page, nh, LANES), 2)
    own_lane = tok == lane
    ones = jnp.ones((LANES, LANES), BF16)
    uu = _suffix_matrix(page, rows_first=True)
    run = run_ref[...]
    acc = acc_ref[...]
    for i in range(npg):
        prod = (k_refs[i][...] * q[None]).reshape(page * nh, LANES).astype(BF16)
        zrep = _dot(prod, ones).reshape(page, nh, LANES)
        z = jnp.sum(jnp.where(own_lane, zrep, 0.0), axis=0) * scale + bias
        lk = -(jnp.maximum(z, 0.0) + jnp.log(1.0 + jnp.exp(-jnp.abs(z))))
        hi = lk.astype(BF16)
        lo = (lk - hi.astype(F32)).astype(BF16)
        suf = _dot(jnp.concatenate([hi, lo], axis=1), uu)
        w = jnp.exp(z + suf + run)
        wsel = jnp.where(own_lane, w[None], 0.0).reshape(page * nh, LANES).astype(BF16)
        wrep = _dot(wsel, ones).reshape(page, nh, LANES)
        acc = acc + jnp.sum(wrep * v_refs[i][...], axis=0)
        run = run + suf[:, 0:1]
    run_ref[...] = run
    acc_ref[...] = acc

    @pl.when(p == pl.num_programs(1) - 1)
    def _():
        o_ref[...] = acc


def _sb_decode(q, cache_k, cache_v, page_table, b_sb, l):
    nb = q.shape[0]
    _, _, page, nh, dh = cache_k.shape
    n_pages = page_table.shape[1]
    npg = math.gcd(DEC_PAGES, n_pages)
    assert nh == SUBLANES and dh == LANES and page == LANES
    bias = jnp.broadcast_to(b_sb[l][:, None], (nh, LANES))

    def page_spec(i):
        return pl.BlockSpec((None, None, page, nh, dh),
                            lambda b, p, pt: (l, pt[b, n_pages - 1 - (p * npg + i)], 0, 0, 0))

    grid_spec = pltpu.PrefetchScalarGridSpec(
        num_scalar_prefetch=1,
        grid=(nb, n_pages // npg),
        in_specs=[pl.BlockSpec((nh, LANES), lambda b, p, pt: (0, 0)),
                  pl.BlockSpec((None, nh, dh), lambda b, p, pt: (b, 0, 0))]
                 + [page_spec(i) for i in range(npg)] * 2,
        out_specs=pl.BlockSpec((None, nh, dh), lambda b, p, pt: (b, 0, 0)),
        scratch_shapes=[pltpu.VMEM((nh, LANES), F32), pltpu.VMEM((nh, dh), F32)],
    )
    o = pl.pallas_call(
        functools.partial(_sb_decode_kernel, page=page, npg=npg, scale=dh ** -0.5),
        grid_spec=grid_spec,
        out_shape=jax.ShapeDtypeStruct((nb, nh, dh), F32),
        compiler_params=_cparams(("parallel", "arbitrary")),
        name="sb_decode",
    )(page_table, bias, q.reshape(nb, nh, dh), *([cache_k] * npg), *([cache_v] * npg))
    return o.reshape(nb, 1, nh * dh).astype(BF16)


def _trunk(path, x, mod_all, wts, state_a, state_b, cache, dims):
    d_model = x.shape[-1]
    depth = wts["w_o"].shape[0]
    c1, c2 = dims["col_rec"], dims["col_att"]
    is_prompt = cache is None
    tm = path.tm
    sas, sbs = [], []
    kbuf = vbuf = None
    for l in range(depth):
        h = _norm_mod(path, x, wts["g_mix"], mod_all, l, 1, 0)
        y1 = _mm_nt(path, h, wts["w_in_rec"], l, c1["end"] // 2, name="in_proj_rec")
        y2, kbuf, vbuf = _in_proj_att(path, h, wts["w_in_att"], l, kbuf, vbuf)
        r = _mm_nt(path, h, wts["w_in_rb"], l, LANES, name="in_proj_r")
        bv, lv, _ = y1.shape
        if is_prompt:
            o_a, s_a = _rec_a(y1, wts["lb_gamma"], wts["gn_a"], state_a[l], l, dims["chunk"])
            o_b, s_b = _rec_b(y1, r, wts["w_gk_p"], wts["b_gk"], wts["gn_b"], state_b[l], l, dims["chunk"], c1["qb"])
            o_c = _sb_prompt(y2, kbuf, vbuf, wts["b_sb"], l, c2["qc"], dims["h_c"], dims["tq"])
        else:
            f1, f2 = y1.reshape(lv, -1), y2.reshape(lv, -1)
            s1 = lambda a, b_: f1[:, c1[a]:c1[b_]]
            o_a, s_a = _dec_a(s1("qa", "fa"), s1("fa", "ia"), s1("ia", "ga"), s1("ga", "qb"),
                              wts["lb_gamma"], wts["gn_a"], state_a[l], l)
            o_b, s_b = _dec_b(s1("qb", "kb"), s1("kb", "vb"), s1("vb", "gb"), s1("gb", "end"),
                              r.reshape(lv, -1)[:, :dims["gk_rank"]], wts["w_gk_t"], wts["b_gk"], wts["gn_b"],
                              state_b[l], l)
            o_c = _sb_decode(f2[:, c2["qc"]:c2["ma"]], cache[0], cache[1], cache[2], wts["b_sb"], l)
            o_a, o_b, o_c = (o.reshape(bv, lv, -1) for o in (o_a, o_b, o_c))
        merged = _merge(path, o_a, o_b, o_c, y2, c2["ma"], wts["w_br"], l, 512)
        x = _mm_resid(path, merged, wts["w_o"], x, mod_all, l, 2, 1024, name="out_proj")
        h2 = _norm_mod(path, x, wts["g_ffn"], mod_all, l, 4, 3)
        act = _swiglu(path, h2, wts["w_gu"], l, 512, tm=path.tm_big)
        x = _mm_resid(path, act, wts["w_down"], x, mod_all, l, 5, 512, name="down_proj")
        sas.append(s_a)
        sbs.append(s_b)
    xf = _final_norm(x, wts["g_final"], tm)
    return xf, kbuf, vbuf, jnp.stack(sas), jnp.stack(sbs)


def kernel(x_prompt, x_sample, cache_k, cache_v, state_hgrn, state_gla, page_table, c_prompt, c_sample, w_ada, b_ada, g_mix, w_in, b_sb, lb_gamma, gn_a, w_gk, b_gk, gn_b, w_br, w_o, g_ffn, w_gu, w_down, g_final):
    bp, seq, d_model = x_prompt.shape
    nb = x_sample.shape[0]
    depth, _, _, h_c, dh_c = cache_k.shape[0], None, None, cache_k.shape[3], cache_k.shape[4]
    h_a, dk_a, dv_a = state_hgrn.shape[2:]
    h_b, dk_b, dv_b = state_gla.shape[2:]
    gk_rank = w_gk.shape[1]
    w_a, w_b, w_c = h_a * dv_a, h_b * dv_b, h_c * dh_c

    def offsets(widths):
        col, off = {}, 0
        for name, wd in widths:
            col[name] = off
            off += wd
        col["end"] = off
        return col

    col_rec = offsets([("qa", h_a * dk_a), ("fa", h_a * dk_a), ("ia", w_a), ("ga", w_a),
                       ("qb", h_b * dk_b), ("kb", h_b * dk_b), ("vb", w_b), ("gb", w_b)])
    col_att = offsets([("qc", w_c), ("ma", d_model), ("mb", d_model), ("mc", d_model)])
    assert d_model == 2 * w_c
    src_rb = col_rec["end"]
    w_in_t = jnp.swapaxes(w_in, 1, 2)
    w_in_rb = jnp.concatenate([w_in_t[:, src_rb:src_rb + gk_rank],
                               jnp.zeros((depth, LANES - gk_rank, d_model), w_in.dtype)], axis=1).astype(BF16)
    w_gk_p = jnp.concatenate([w_gk, jnp.zeros((depth, LANES - gk_rank, w_gk.shape[-1]), w_gk.dtype)],
                             axis=1).astype(BF16)
    wts = {
        "w_in_rec": w_in_t[:, :src_rb].astype(BF16), "w_in_att": w_in_t[:, src_rb + gk_rank:].astype(BF16),
        "w_in_rb": w_in_rb,
        "w_br": w_br.astype(BF16), "w_o": w_o.astype(BF16), "w_gu": w_gu.astype(BF16),
        "w_down": w_down.astype(BF16), "w_gk_p": w_gk_p,
        "w_gk_t": jnp.swapaxes(w_gk, 1, 2).reshape(depth, h_b, dk_b, gk_rank),
        "g_mix": g_mix, "g_ffn": g_ffn, "g_final": g_final, "b_sb": b_sb, "lb_gamma": lb_gamma,
        "gn_a": gn_a, "gn_b": gn_b, "b_gk": b_gk,
    }
    dims = {"col_rec": col_rec, "col_att": col_att, "chunk": 256, "tq": 256, "h_c": h_c, "gk_rank": gk_rank}

    c_all = jnp.concatenate([c_sample, c_prompt, jnp.zeros((16 - nb - bp, d_model), F32)], axis=0)
    mod_all = _ada_mod(c_all, w_ada, b_ada)

    prompt = _Path(bp, seq, min(1024, seq), per_row_mod=False, mod_row0=nb)
    sample = _Path(1, nb, nb, per_row_mod=True, mod_row0=0)

    zeros_a = jnp.zeros((depth, bp) + state_hgrn.shape[2:], state_hgrn.dtype)
    zeros_b = jnp.zeros((depth, bp) + state_gla.shape[2:], state_gla.dtype)
    y_p, k_p, v_p, sa_p, sb_p = _trunk(prompt, x_prompt, mod_all, wts, zeros_a, zeros_b, None, dims)

    cache = (cache_k, cache_v, page_table)
    y_s, k_s, v_s, sa_s, sb_s = _trunk(sample, x_sample.reshape(1, nb, d_model), mod_all, wts,
                                       state_hgrn, state_gla, cache, dims)

    return (y_p, y_s.reshape(nb, 1, d_model),
            k_p.reshape(depth, bp, seq, h_c, dh_c), v_p.reshape(depth, bp, seq, h_c, dh_c),
            k_s.reshape(depth, nb, 1, h_c, dh_c), v_s.reshape(depth, nb, 1, h_c, dh_c),
            sa_p, sa_s, sb_p, sb_s)
```

```python
import functools
import math

import jax
import jax.numpy as jnp
from jax import lax
from jax.experimental import pallas as pl
from jax.experimental.pallas import tpu as pltpu

F32 = jnp.float32
BF16 = jnp.bfloat16

EPS = 1e-6
GK_NORM = 16.0
N_MOD = 6
LANES = 128
SUBLANES = 8
VMEM_LIMIT = 56 * 1024 * 1024
DEC_PAGES = 16


def _cparams(sem):
    return pltpu.CompilerParams(dimension_semantics=sem, vmem_limit_bytes=VMEM_LIMIT)


def _silu(x):
    return x * jax.nn.sigmoid(x)


def _log_sigmoid(x):
    return jnp.minimum(x, 0.0) - jnp.log1p(jnp.exp(-jnp.abs(x)))


def _logaddexp(a, b):
    m = jnp.maximum(a, b)
    return m + jnp.log1p(jnp.exp(-jnp.abs(a - b)))


def _dot(a, b):
    return jnp.dot(a, b, preferred_element_type=F32)


def _dot_nt(a, b):
    return lax.dot_general(a, b, (((1,), (1,)), ((), ())), preferred_element_type=F32)


def _dot_tn(a, b):
    return lax.dot_general(a, b, (((0,), (0,)), ((), ())), preferred_element_type=F32)


class _Path:
    def __init__(self, bv, lv, tm, per_row_mod, mod_row0):
        self.bv, self.lv, self.tm = bv, lv, tm
        self.per_row_mod = per_row_mod
        self.mod_row0 = mod_row0

    def grid_rows(self):
        return (self.bv, self.lv // self.tm)

    def mod_operand(self, mod_all):
        if self.per_row_mod:
            return mod_all
        d, r, w = mod_all.shape
        return mod_all.reshape(d, r, 1, w)

    def mod_spec(self, l, k, d_model, tn=None, with_j=False):
        tn = d_model if tn is None else tn
        per = d_model // tn
        if self.per_row_mod:
            if with_j:
                return pl.BlockSpec((None, self.tm, tn), lambda b, i, j: (l, i, k * per + j))
            return pl.BlockSpec((None, self.tm, tn), lambda b, i: (l, i, k * per))
        r0 = self.mod_row0
        if with_j:
            return pl.BlockSpec((None, None, 1, tn), lambda b, i, j: (l, r0 + b, 0, k * per + j))
        return pl.BlockSpec((None, None, 1, tn), lambda b, i: (l, r0 + b, 0, k * per))


def _ada_kernel(c_ref, w_ref, b_ref, o_ref):
    a = _silu(c_ref[...]).astype(BF16)
    o_ref[...] = _dot(a, w_ref[...].astype(BF16)) + b_ref[...]


def _ada_mod(c_all, w_ada, b_ada, tn=1024):
    depth, d, n = w_ada.shape
    rows = c_all.shape[0]
    return pl.pallas_call(
        _ada_kernel,
        grid=(depth, n // tn),
        in_specs=[
            pl.BlockSpec((rows, d), lambda l, j: (0, 0)),
            pl.BlockSpec((None, d, tn), lambda l, j: (l, 0, j)),
            pl.BlockSpec((None, 1, tn), lambda l, j: (l, 0, j)),
        ],
        out_specs=pl.BlockSpec((None, rows, tn), lambda l, j: (l, 0, j)),
        out_shape=jax.ShapeDtypeStruct((depth, rows, n), F32),
        compiler_params=_cparams(("parallel", "parallel")),
        name="ada_mod",
    )(c_all, w_ada, b_ada.reshape(depth, 1, n))


def _norm_mod_kernel(x_ref, g_ref, sc_ref, sh_ref, o_ref):
    x = x_ref[...]
    y = x * lax.rsqrt(jnp.mean(x * x, axis=-1, keepdims=True) + EPS) * g_ref[...]
    o_ref[...] = (y * (1.0 + sc_ref[...]) + sh_ref[...]).astype(o_ref.dtype)


def _norm_kernel(x_ref, g_ref, o_ref):
    x = x_ref[...]
    y = x * lax.rsqrt(jnp.mean(x * x, axis=-1, keepdims=True) + EPS) * g_ref[...]
    o_ref[...] = y.astype(o_ref.dtype)


def _norm_mod(path, x, g, mod_all, l, k_sc, k_sh, tm=None):
    bv, lv, d = x.shape
    tm = path.tm if tm is None else tm
    p = _Path(bv, lv, tm, path.per_row_mod, path.mod_row0)
    depth = g.shape[0]
    mod = p.mod_operand(mod_all)
    return pl.pallas_call(
        _norm_mod_kernel,
        grid=p.grid_rows(),
        in_specs=[
            pl.BlockSpec((None, tm, d), lambda b, i: (b, i, 0)),
            pl.BlockSpec((None, 1, d), lambda b, i: (l, 0, 0)),
            p.mod_spec(l, k_sc, d),
            p.mod_spec(l, k_sh, d),
        ],
        out_specs=pl.BlockSpec((None, tm, d), lambda b, i: (b, i, 0)),
        out_shape=jax.ShapeDtypeStruct((bv, lv, d), BF16),
        compiler_params=_cparams(("parallel", "parallel")),
        name="norm_mod",
    )(x, g.reshape(depth, 1, d), mod, mod)


def _final_norm(x, g, tm):
    bv, lv, d = x.shape
    return pl.pallas_call(
        _norm_kernel,
        grid=(bv, lv // tm),
        in_specs=[
            pl.BlockSpec((None, tm, d), lambda b, i: (b, i, 0)),
            pl.BlockSpec((1, d), lambda b, i: (0, 0)),
        ],
        out_specs=pl.BlockSpec((None, tm, d), lambda b, i: (b, i, 0)),
        out_shape=jax.ShapeDtypeStruct((bv, lv, d), F32),
        compiler_params=_cparams(("parallel", "parallel")),
        name="final_norm",
    )(x, g.reshape(1, d))


def _mm_nt_kernel(a_ref, w_ref, o_ref):
    o_ref[...] = _dot_nt(a_ref[...], w_ref[...]).astype(o_ref.dtype)


def _mm_nt(path, a, w_t, l, tn, tm=None, out_dtype=F32, name="mm"):
    bv, lv, k = a.shape
    ncols = w_t.shape[1]
    tm = path.tm if tm is None else tm
    assert ncols % tn == 0 and lv % tm == 0
    return pl.pallas_call(
        _mm_nt_kernel,
        grid=(bv, lv // tm, ncols // tn),
        in_specs=[
            pl.BlockSpec((None, tm, k), lambda b, i, j: (b, i, 0)),
            pl.BlockSpec((None, tn, k), lambda b, i, j: (l, j, 0)),
        ],
        out_specs=pl.BlockSpec((None, tm, tn), lambda b, i, j: (b, i, j)),
        out_shape=jax.ShapeDtypeStruct((bv, lv, ncols), out_dtype),
        compiler_params=_cparams(("parallel", "parallel", "arbitrary")),
        name=name,
    )(a, w_t)


def _in_proj_att_kernel(*refs, aliased):
    if aliased:
        a_ref, w_ref, _, _, y_ref, k_ref, v_ref = refs
    else:
        a_ref, w_ref, y_ref, k_ref, v_ref = refs
    j = pl.program_id(2)

    @pl.when(j == 1)
    def _():
        k_ref[...] = _dot_nt(a_ref[...], w_ref[...])

    @pl.when(j == 2)
    def _():
        v_ref[...] = _dot_nt(a_ref[...], w_ref[...])

    @pl.when(j == 0)
    def _():
        y_ref[...] = _dot_nt(a_ref[...], w_ref[...])


def _in_proj_att(path, a, w_t, l, kbuf, vbuf, tn, tm=None):
    bv, lv, k = a.shape
    depth = w_t.shape[0]
    tm = path.tm if tm is None else tm
    aliased = kbuf is not None
    kv_shape = jax.ShapeDtypeStruct((depth, bv, lv, tn), F32)
    kv_spec = pl.BlockSpec((None, None, tm, tn), lambda b, i, j: (l, b, i, 0))
    any_spec = pl.BlockSpec(memory_space=pl.ANY)
    return pl.pallas_call(
        functools.partial(_in_proj_att_kernel, aliased=aliased),
        grid=(bv, lv // tm, 3),
        in_specs=[
            pl.BlockSpec((None, tm, k), lambda b, i, j: (b, i, 0)),
            pl.BlockSpec((None, tn, k), lambda b, i, j: (l, j, 0)),
        ] + ([any_spec, any_spec] if aliased else []),
        out_specs=[pl.BlockSpec((None, tm, tn), lambda b, i, j: (b, i, 0)), kv_spec, kv_spec],
        out_shape=[jax.ShapeDtypeStruct((bv, lv, tn), F32), kv_shape, kv_shape],
        input_output_aliases={2: 1, 3: 2} if aliased else {},
        compiler_params=_cparams(("parallel", "parallel", "arbitrary")),
        name="in_proj_att",
    )(a, w_t, *((kbuf, vbuf) if aliased else ()))


def _mm_resid_kernel(a_ref, w_ref, x_ref, gt_ref, o_ref):
    o_ref[...] = x_ref[...] + gt_ref[...] * _dot(a_ref[...], w_ref[...])


def _mm_resid(path, a, w, x, mod_all, l, k_gt, tn, tm=None, name="mm_resid"):
    bv, lv, k = a.shape
    d = w.shape[-1]
    tm = path.tm if tm is None else tm
    p = _Path(bv, lv, tm, path.per_row_mod, path.mod_row0)
    return pl.pallas_call(
        _mm_resid_kernel,
        grid=(bv, lv // tm, d // tn),
        in_specs=[
            pl.BlockSpec((None, tm, k), lambda b, i, j: (b, i, 0)),
            pl.BlockSpec((None, k, tn), lambda b, i, j: (l, 0, j)),
            pl.BlockSpec((None, tm, tn), lambda b, i, j: (b, i, j)),
            p.mod_spec(l, k_gt, d, tn=tn, with_j=True),
        ],
        out_specs=pl.BlockSpec((None, tm, tn), lambda b, i, j: (b, i, j)),
        out_shape=jax.ShapeDtypeStruct((bv, lv, d), F32),
        compiler_params=_cparams(("parallel", "parallel", "arbitrary")),
        name=name,
    )(a, w, x, p.mod_operand(mod_all))


def _swiglu_kernel(a_ref, wg_ref, wu_ref, o_ref):
    a = a_ref[...]
    o_ref[...] = (_silu(_dot(a, wg_ref[...])) * _dot(a, wu_ref[...])).astype(o_ref.dtype)


def _swiglu(path, a, w_gu, l, tn, tm=None):
    bv, lv, k = a.shape
    d_ff = w_gu.shape[-1] // 2
    tm = path.tm if tm is None else tm
    nj = d_ff // tn
    assert d_ff % tn == 0
    return pl.pallas_call(
        _swiglu_kernel,
        grid=(bv, lv // tm, nj),
        in_specs=[
            pl.BlockSpec((None, tm, k), lambda b, i, j: (b, i, 0)),
            pl.BlockSpec((None, k, tn), lambda b, i, j: (l, 0, j)),
            pl.BlockSpec((None, k, tn), lambda b, i, j: (l, 0, nj + j)),
        ],
        out_specs=pl.BlockSpec((None, tm, tn), lambda b, i, j: (b, i, j)),
        out_shape=jax.ShapeDtypeStruct((bv, lv, d_ff), BF16),
        compiler_params=_cparams(("parallel", "parallel", "arbitrary")),
        name="swiglu",
    )(a, w_gu, w_gu)


def _merge_kernel(h_ref, oa_ref, ob_ref, oc_ref, ga_ref, gb_ref, gc_ref, wa_ref, wb_ref, wc_ref, o_ref):
    h = h_ref[...]
    m = jax.nn.sigmoid(_dot_nt(h, ga_ref[...])) * _dot(oa_ref[...], wa_ref[...])
    m = m + jax.nn.sigmoid(_dot_nt(h, gb_ref[...])) * _dot(ob_ref[...], wb_ref[...])
    m = m + jax.nn.sigmoid(_dot_nt(h, gc_ref[...])) * _dot(oc_ref[...], wc_ref[...])
    o_ref[...] = m.astype(o_ref.dtype)


def _merge(path, h, o_a, o_b, o_c, w_t, gate_row0, w_br, l, tn, tm=None):
    bv, lv, wa = o_a.shape
    wb, wc = o_b.shape[-1], o_c.shape[-1]
    k = h.shape[-1]
    d = w_br.shape[-1]
    tm = path.tm if tm is None else tm
    g0 = gate_row0 // tn
    per = d // tn
    assert gate_row0 % tn == 0 and wb == wa and wc == 2 * wa
    return pl.pallas_call(
        _merge_kernel,
        grid=(bv, lv // tm, d // tn),
        in_specs=[
            pl.BlockSpec((None, tm, k), lambda b, i, j: (b, i, 0)),
            pl.BlockSpec((None, tm, wa), lambda b, i, j: (b, i, 0)),
            pl.BlockSpec((None, tm, wb), lambda b, i, j: (b, i, 0)),
            pl.BlockSpec((None, tm, wc), lambda b, i, j: (b, i, 0)),
            pl.BlockSpec((None, tn, k), lambda b, i, j: (l, g0 + j, 0)),
            pl.BlockSpec((None, tn, k), lambda b, i, j: (l, g0 + per + j, 0)),
            pl.BlockSpec((None, tn, k), lambda b, i, j: (l, g0 + 2 * per + j, 0)),
            pl.BlockSpec((None, wa, tn), lambda b, i, j: (l, 0, j)),
            pl.BlockSpec((None, wb, tn), lambda b, i, j: (l, 1, j)),
            pl.BlockSpec((None, wc, tn), lambda b, i, j: (l, 1, j)),
        ],
        out_specs=pl.BlockSpec((None, tm, tn), lambda b, i, j: (b, i, j)),
        out_shape=jax.ShapeDtypeStruct((bv, lv, d), BF16),
        compiler_params=_cparams(("parallel", "parallel", "arbitrary")),
        name="merge",
    )(h, o_a, o_b, o_c, w_t, w_t, w_t, w_br, w_br, w_br)


def _lb_from_gamma(gam, l):
    depth = len(gam)
    m = gam[0]
    for i in range(1, depth):
        m = jnp.maximum(m, gam[i])
    e = [jnp.exp(gam[i] - m) for i in range(depth)]
    tot = e[0]
    for i in range(1, depth):
        tot = tot + e[i]
    sm = [ei / tot for ei in e]
    cs = [sm[0]]
    for i in range(1, depth):
        cs.append(cs[-1] + sm[i])
    return cs[l] - cs[0]


def _rec_core(qq, kk, logf, v_ref_tile, g_tile, gn, st_ref, o_ref, *, chunk, n_groups, hpg):
    c = chunk
    wq = n_groups * LANES
    dk = LANES // hpg
    n_levels = int(math.log2(c))
    row = lax.broadcasted_iota(jnp.int32, (c, wq), 0)
    ti = lax.broadcasted_iota(jnp.int32, (c, c), 0)
    si = lax.broadcasted_iota(jnp.int32, (c, c), 1)
    x = jnp.bitwise_xor(ti, si)
    lvl = jnp.zeros((c, c), jnp.int32)
    for lev in range(1, n_levels + 1):
        lvl = lvl + jnp.where(x >= (1 << (lev - 1)), 1, 0)
    lvl = jnp.where(ti > si, lvl, -1)
    lane = lax.broadcasted_iota(jnp.int32, (c, LANES), 1)

    def head_mask(a, sub):
        if hpg == 1:
            return a
        return jnp.where(lane < dk, a, 0.0) if sub == 0 else jnp.where(lane >= dk, a, 0.0)

    pre = logf
    tot = logf
    ql, kl = [], []
    for lev in range(1, n_levels + 1):
        half = 1 << (lev - 1)
        upper = (row & half) != 0
        e = jnp.exp(jnp.where(upper, pre, tot - pre))
        ql.append(qq * e)
        kl.append(kk * e)
        up = pltpu.roll(tot, half, 0)
        dn = pltpu.roll(tot, c - half, 0)
        pre = pre + jnp.where(upper, up, 0.0)
        tot = tot + jnp.where(upper, up, dn)
    q_in = qq * jnp.exp(pre)
    k_out = kk * jnp.exp(tot - pre)
    d_all = jnp.exp(tot[0:1, :])
    qk = qq * kk

    for g in range(n_groups):
        gs = slice(g * LANES, (g + 1) * LANES)
        st = st_ref[g]
        st_bf = st.astype(BF16)
        st_new = st * d_all[:, gs]
        for sub in range(hpg):
            h = g * hpg + sub
            hs = slice(h * LANES, (h + 1) * LANES)
            v_bf = v_ref_tile[:, hs].astype(BF16)
            att = jnp.zeros((c, c), F32)
            for lev in range(1, n_levels + 1):
                p = _dot_nt(head_mask(ql[lev - 1][:, gs], sub).astype(BF16), kl[lev - 1][:, gs].astype(BF16))
                att = jnp.where(lvl == lev, p, att)
            diag = jnp.sum(head_mask(qk[:, gs], sub), axis=-1, keepdims=True)
            att = jnp.where(x == 0, diag, att)
            o = _dot_nt(head_mask(q_in[:, gs], sub).astype(BF16), st_bf) + _dot(att.astype(BF16), v_bf)
            st_new = st_new + _dot_tn(v_bf, head_mask(k_out[:, gs], sub).astype(BF16))
            y = o * lax.rsqrt(jnp.mean(o * o, axis=-1, keepdims=True) + EPS) * gn
            o_ref[:, hs] = (y * _silu(g_tile[:, hs])).astype(o_ref.dtype)
        st_ref[g] = st_new


def _state_in(s0_ref, st_ref, n_groups, hpg):
    dk = LANES // hpg
    for g in range(n_groups):
        blk = jnp.concatenate([s0_ref[g * hpg + sub] for sub in range(hpg)], axis=0) if hpg > 1 else s0_ref[g]
        st_ref[g] = blk.T


def _state_out(st_ref, sout_ref, n_groups, hpg):
    dk = LANES // hpg
    for g in range(n_groups):
        t = st_ref[g].T
        for sub in range(hpg):
            sout_ref[g * hpg + sub] = t[sub * dk:(sub + 1) * dk, :]


def _rec_a_kernel(q_ref, f_ref, i_ref, g_ref, lbg_ref, gn_ref, s0_ref, o_ref, sout_ref, st_ref, *, l, chunk):
    ci = pl.program_id(1)

    @pl.when(ci == 0)
    def _():
        _state_in(s0_ref, st_ref, 4, 1)

    lb = _lb_from_gamma([lbg_ref[i:i + 1, :] for i in range(lbg_ref.shape[0])], l)
    f = f_ref[...]
    logf = _logaddexp(jnp.log(lb), jnp.log1p(-lb) + _log_sigmoid(f))
    kk = (1.0 - lb) * jax.nn.sigmoid(-f)
    qq = _silu(q_ref[...]) * (LANES ** -0.5)
    _rec_core(qq, kk, logf, i_ref, g_ref[...], gn_ref[...], st_ref, o_ref, chunk=chunk, n_groups=4, hpg=1)

    @pl.when(ci == pl.num_programs(1) - 1)
    def _():
        _state_out(st_ref, sout_ref, 4, 1)


def _rec_b_kernel(q_ref, k_ref, v_ref, g_ref, r_ref, wgk_ref, bgk_ref, gn_ref, s0_ref, o_ref, sout_ref, st_ref,
                  *, chunk):
    ci = pl.program_id(1)

    @pl.when(ci == 0)
    def _():
        _state_in(s0_ref, st_ref, 2, 2)

    gk = _dot(r_ref[...].astype(BF16), wgk_ref[...]) + bgk_ref[...]
    logf = _log_sigmoid(gk) * (1.0 / GK_NORM)
    qq = q_ref[...] * ((LANES // 2) ** -0.5)
    _rec_core(qq, k_ref[...], logf, v_ref, g_ref[...], gn_ref[...], st_ref, o_ref, chunk=chunk, n_groups=2, hpg=2)

    @pl.when(ci == pl.num_programs(1) - 1)
    def _():
        _state_out(st_ref, sout_ref, 2, 2)


def _rec_a(y, lb_gamma, gn_a, s0, l, chunk):
    b, L, _ = y.shape
    depth = lb_gamma.shape[0]
    w = 4 * LANES
    col = lambda k: pl.BlockSpec((None, chunk, w), lambda bi, ci: (bi, ci, k))
    return pl.pallas_call(
        functools.partial(_rec_a_kernel, l=l, chunk=chunk),
        grid=(b, L // chunk),
        in_specs=[
            col(0), col(1), col(2), col(3),
            pl.BlockSpec((depth, w), lambda bi, ci: (0, 0)),
            pl.BlockSpec((None, 1, LANES), lambda bi, ci: (l, 0, 0)),
            pl.BlockSpec((None, 4, LANES, LANES), lambda bi, ci: (bi, 0, 0, 0)),
        ],
        out_specs=[
            pl.BlockSpec((None, chunk, w), lambda bi, ci: (bi, ci, 0)),
            pl.BlockSpec((None, 4, LANES, LANES), lambda bi, ci: (bi, 0, 0, 0)),
        ],
        out_shape=[jax.ShapeDtypeStruct((b, L, w), BF16), jax.ShapeDtypeStruct(s0.shape, F32)],
        scratch_shapes=[pltpu.VMEM((4, LANES, LANES), F32)],
        compiler_params=_cparams(("parallel", "arbitrary")),
        name="rec_hgrn",
    )(y, y, y, y, lb_gamma, gn_a.reshape(depth, 1, LANES), s0)


def _rec_b(y, r, wgk_p, b_gk, gn_b, s0, l, chunk, col_b):
    b, L, _ = y.shape
    depth = gn_b.shape[0]
    c256 = col_b // 256
    c512 = (col_b + 512) // 512
    return pl.pallas_call(
        functools.partial(_rec_b_kernel, chunk=chunk),
        grid=(b, L // chunk),
        in_specs=[
            pl.BlockSpec((None, chunk, 256), lambda bi, ci: (bi, ci, c256)),
            pl.BlockSpec((None, chunk, 256), lambda bi, ci: (bi, ci, c256 + 1)),
            pl.BlockSpec((None, chunk, 512), lambda bi, ci: (bi, ci, c512)),
            pl.BlockSpec((None, chunk, 512), lambda bi, ci: (bi, ci, c512 + 1)),
            pl.BlockSpec((None, chunk, LANES), lambda bi, ci: (bi, ci, 0)),
            pl.BlockSpec((None, LANES, 256), lambda bi, ci: (l, 0, 0)),
            pl.BlockSpec((None, 1, 256), lambda bi, ci: (l, 0, 0)),
            pl.BlockSpec((None, 1, LANES), lambda bi, ci: (l, 0, 0)),
            pl.BlockSpec((None, 4, 64, LANES), lambda bi, ci: (bi, 0, 0, 0)),
        ],
        out_specs=[
            pl.BlockSpec((None, chunk, 512), lambda bi, ci: (bi, ci, 0)),
            pl.BlockSpec((None, 4, 64, LANES), lambda bi, ci: (bi, 0, 0, 0)),
        ],
        out_shape=[jax.ShapeDtypeStruct((b, L, 512), BF16), jax.ShapeDtypeStruct(s0.shape, F32)],
        scratch_shapes=[pltpu.VMEM((2, LANES, LANES), F32)],
        compiler_params=_cparams(("parallel", "arbitrary")),
        name="rec_gla",
    )(y, y, y, y, r, wgk_p, b_gk.reshape(depth, 1, 256), gn_b.reshape(depth, 1, LANES), s0)


def _dec_finish(s_new, q_col, g_row, gn, o_ref, h):
    o = jnp.sum(q_col * s_new, axis=0, keepdims=True)
    y = o * lax.rsqrt(jnp.mean(o * o, axis=-1, keepdims=True) + EPS) * gn
    o_ref[h] = y * _silu(g_row)


def _dec_a_kernel(q_ref, f_ref, i_ref, g_ref, lbg_ref, gn_ref, s0_ref, o_ref, sout_ref, *, l):
    lb_all = _lb_from_gamma([lbg_ref[i] for i in range(lbg_ref.shape[0])], l)
    for h in range(4):
        lb = lb_all[h]
        f = f_ref[h]
        logf = _logaddexp(jnp.log(lb), jnp.log1p(-lb) + _log_sigmoid(f))
        kk = (1.0 - lb) * jax.nn.sigmoid(-f)
        qq = _silu(q_ref[h]) * (LANES ** -0.5)
        s_new = jnp.exp(logf) * s0_ref[h] + kk * i_ref[h]
        sout_ref[h] = s_new
        _dec_finish(s_new, qq, g_ref[h], gn_ref[...], o_ref, h)


def _dec_b_kernel(q_ref, k_ref, v_ref, g_ref, r_ref, wgk_ref, bgk_ref, gn_ref, s0_ref, o_ref, sout_ref):
    r = r_ref[...]
    for h in range(4):
        gk = jnp.sum(wgk_ref[h] * r, axis=-1, keepdims=True) + bgk_ref[h]
        logf = _log_sigmoid(gk) * (1.0 / GK_NORM)
        qq = q_ref[h] * ((LANES // 2) ** -0.5)
        s_new = jnp.exp(logf) * s0_ref[h] + k_ref[h] * v_ref[h]
        sout_ref[h] = s_new
        _dec_finish(s_new, qq, g_ref[h], gn_ref[...], o_ref, h)


def _dec_a(qa, fa, ia, ga, lb_gamma, gn_a, s0, l):
    nb = qa.shape[0]
    depth = lb_gamma.shape[0]
    colspec = pl.BlockSpec((None, 4, LANES, 1), lambda b: (b, 0, 0, 0))
    rowspec = pl.BlockSpec((None, 4, 1, LANES), lambda b: (b, 0, 0, 0))
    stspec = pl.BlockSpec((None, 4, LANES, LANES), lambda b: (b, 0, 0, 0))
    o, s = pl.pallas_call(
        functools.partial(_dec_a_kernel, l=l),
        grid=(nb,),
        in_specs=[colspec, colspec, rowspec, rowspec,
                  pl.BlockSpec((depth, 4, LANES, 1), lambda b: (0, 0, 0, 0)),
                  pl.BlockSpec((None, 1, LANES), lambda b: (l, 0, 0)),
                  stspec],
        out_specs=[rowspec, stspec],
        out_shape=[jax.ShapeDtypeStruct((nb, 4, 1, LANES), F32), jax.ShapeDtypeStruct(s0.shape, F32)],
        compiler_params=_cparams(("parallel",)),
        name="dec_hgrn",
    )(qa.reshape(nb, 4, LANES, 1), fa.reshape(nb, 4, LANES, 1), ia.reshape(nb, 4, 1, LANES),
      ga.reshape(nb, 4, 1, LANES), lb_gamma.reshape(depth, 4, LANES, 1), gn_a.reshape(depth, 1, LANES), s0)
    return o.reshape(nb, 1, 4 * LANES).astype(BF16), s


def _dec_b(qb, kb, vb, gb, rb, wgk_t, b_gk, gn_b, s0, l):
    nb = qb.shape[0]
    depth = gn_b.shape[0]
    rank = rb.shape[-1]
    colspec = pl.BlockSpec((None, 4, 64, 1), lambda b: (b, 0, 0, 0))
    rowspec = pl.BlockSpec((None, 4, 1, LANES), lambda b: (b, 0, 0, 0))
    stspec = pl.BlockSpec((None, 4, 64, LANES), lambda b: (b, 0, 0, 0))
    o, s = pl.pallas_call(
        _dec_b_kernel,
        grid=(nb,),
        in_specs=[colspec, colspec, rowspec, rowspec,
                  pl.BlockSpec((None, 1, rank), lambda b: (b, 0, 0)),
                  pl.BlockSpec((None, 4, 64, rank), lambda b: (l, 0, 0, 0)),
                  pl.BlockSpec((None, 4, 64, 1), lambda b: (l, 0, 0, 0)),
                  pl.BlockSpec((None, 1, LANES), lambda b: (l, 0, 0)),
                  stspec],
        out_specs=[rowspec, stspec],
        out_shape=[jax.ShapeDtypeStruct((nb, 4, 1, LANES), F32), jax.ShapeDtypeStruct(s0.shape, F32)],
        compiler_params=_cparams(("parallel",)),
        name="dec_gla",
    )(qb.reshape(nb, 4, 64, 1), kb.reshape(nb, 4, 64, 1), vb.reshape(nb, 4, 1, LANES),
      gb.reshape(nb, 4, 1, LANES), rb.reshape(nb, 1, rank), wgk_t, b_gk.reshape(depth, 4, 64, 1),
      gn_b.reshape(depth, 1, LANES), s0)
    return o.reshape(nb, 1, 4 * LANES).astype(BF16), s


def _suffix_matrix(n, rows_first):
    a = lax.broadcasted_iota(jnp.int32, (n, n), 0)
    b = lax.broadcasted_iota(jnp.int32, (n, n), 1)
    if rows_first:
        m = jnp.where(a >= b, 1.0, 0.0).astype(BF16)
        return jnp.concatenate([m, m], axis=0)
    m = jnp.where(b >= a, 1.0, 0.0).astype(BF16)
    return jnp.concatenate([m, m], axis=1)


def _sb_prompt_kernel(bias_ref, qa_ref, qb_ref, k_ref, v_ref, oa_ref, ob_ref,
                      kbf_ref, vbf_ref, uu_ref, z_ref, sp_ref, suf_ref, *, l, tq, nq, scale, unroll):
    h = pl.program_id(1)
    i = pl.program_id(2)
    n_off = nq - 1

    @pl.when(i == 0)
    def _():
        kbf_ref[...] = k_ref[...].astype(BF16)
        vbf_ref[...] = v_ref[...].astype(BF16)
        uu_ref[...] = _suffix_matrix(tq, rows_first=True)[0:tq]

    bias = bias_ref[l, h]
    qa = qa_ref[...].astype(BF16)
    qb = qb_ref[...].astype(BF16)
    uu = uu_ref[...]
    ti = lax.broadcasted_iota(jnp.int32, (tq, tq), 0)
    si = lax.broadcasted_iota(jnp.int32, (tq, tq), 1)
    causal = si < ti

    def off_kb(s):
        return jnp.where(s < i, i - 1 - s, nq - 2 - s)

    def logits(slot, q, kb, masked):
        start = pl.multiple_of(kb * tq, tq)
        z = _dot_nt(q, kbf_ref[pl.ds(start, tq), :]) * scale + bias
        neg_abs = lax.bitcast_convert_type(
            lax.bitcast_convert_type(z, jnp.uint32) | jnp.uint32(0x80000000), F32)
        sp = jnp.maximum(z, 0.0) + jnp.log(1.0 + jnp.exp(neg_abs))
        if masked:
            sp = jnp.where(causal, sp, 0.0)
        z_ref[slot] = z
        sp_ref[slot] = sp.astype(BF16)

    def suffix(slot):
        suf_ref[slot] = _dot(sp_ref[slot], uu)

    def weigh(slot, kb, run, acc, diagonal):
        start = pl.multiple_of(kb * tq, tq)
        suf = suf_ref[slot]
        if diagonal:
            w = jnp.where(causal, jnp.exp(z_ref[slot] - suf), 0.0)
            return suf[:, 0:1], _dot(w.astype(BF16), vbf_ref[pl.ds(start, tq), :])
        w = jnp.exp(z_ref[slot] - suf - run)
        return run + suf[:, 0:1], acc + _dot(w.astype(BF16), vbf_ref[pl.ds(start, tq), :])

    def sweep(diagonals, fn):
        diagonals()
        for u in range(unroll):
            fn(u)

        def body(j, c):
            for u in range(unroll):
                fn(unroll * j + u)
            return c
        lax.fori_loop(1, n_off // unroll, body, 0)

    sweep(lambda: (logits(0, qa, i, True), logits(1, qb, nq - 1 - i, True)),
          lambda s: logits(2 + s, jnp.where(s < i, qa, qb), off_kb(s), False))
    sweep(lambda: (suffix(0), suffix(1)), lambda s: suffix(2 + s))

    run_a, acc_a = weigh(0, i, None, None, True)
    run_b, acc_b = weigh(1, nq - 1 - i, None, None, True)

    def chain(s, c):
        run, acc, out_a = c
        at_b = s == i
        out_a = jnp.where(at_b, acc, out_a)
        run = jnp.where(at_b, run_b, run)
        acc = jnp.where(at_b, acc_b, acc)
        run, acc = weigh(2 + s, off_kb(s), run, acc, False)
        return run, acc, out_a

    def body3(j, c):
        for u in range(unroll):
            c = chain(unroll * j + u, c)
        return c

    c = (run_a, acc_a, acc_a)
    for u in range(unroll):
        c = chain(u, c)
    _, acc, out_a = lax.fori_loop(1, n_off // unroll, body3, c)
    oa_ref[...] = out_a.astype(oa_ref.dtype)
    ob_ref[...] = acc.astype(ob_ref.dtype)


def _sb_prompt(y, kbuf, vbuf, b_sb, l, col_q, n_heads, tq):
    b, L, _ = y.shape
    cq = col_q // LANES
    nq = L // tq
    n_off = nq - 1
    assert nq % 2 == 0
    unroll = max(u for u in (1, 2, 3, 4, 5) if n_off % u == 0)
    half = nq // 2
    n_slots = nq + 1
    o_lo, o_hi = pl.pallas_call(
        functools.partial(_sb_prompt_kernel, l=l, tq=tq, nq=nq, scale=LANES ** -0.5, unroll=unroll),
        grid=(b, n_heads, half),
        in_specs=[
            pl.BlockSpec(memory_space=pltpu.SMEM),
            pl.BlockSpec((None, tq, LANES), lambda bi, h, i: (bi, i, cq + h)),
            pl.BlockSpec((None, tq, LANES), lambda bi, h, i: (bi, nq - 1 - i, cq + h)),
            pl.BlockSpec((None, None, L, LANES), lambda bi, h, i: (l, bi, 0, h)),
            pl.BlockSpec((None, None, L, LANES), lambda bi, h, i: (l, bi, 0, h)),
        ],
        out_specs=[pl.BlockSpec((None, tq, LANES), lambda bi, h, i: (bi, i, h)),
                   pl.BlockSpec((None, tq, LANES), lambda bi, h, i: (bi, half - 1 - i, h))],
        out_shape=[jax.ShapeDtypeStruct((b, L // 2, n_heads * LANES), BF16)] * 2,
        scratch_shapes=[pltpu.VMEM((L, LANES), BF16), pltpu.VMEM((L, LANES), BF16),
                        pltpu.VMEM((tq, tq), BF16),
                        pltpu.VMEM((n_slots, tq, tq), F32), pltpu.VMEM((n_slots, tq, tq), BF16),
                        pltpu.VMEM((n_slots, tq, tq), F32)],
        compiler_params=_cparams(("parallel", "parallel", "arbitrary")),
        name="sb_prompt",
    )(b_sb, y, y, kbuf, vbuf)
    return jnp.concatenate([o_lo, o_hi], axis=1)


def _sb_decode_kernel(pt_ref, bias_ref, q_ref, *refs, page, npg, scale):
    k_refs, v_refs = refs[:npg], refs[npg:2 * npg]
    o_ref, run_ref, acc_ref = refs[2 * npg:]
    p = pl.program_id(1)

    @pl.when(p == 0)
    def _():
        run_ref[...] = jnp.zeros_like(run_ref)
        acc_ref[...] = jnp.zeros_like(acc_ref)

    q = q_ref[...]
    bias = bias_ref[...]
    nh = q.shape[0]
    tok = lax.broadcasted_iota(jnp.int32, (page, nh, LANES), 0)
    lane = lax.broadcasted_iota(jnp.int32, (page, nh, LANES), 2)
    own_lane = tok == lane
    ones = jnp.ones((LANES, LANES), BF16)
    uu = _suffix_matrix(page, rows_first=True)
    run = run_ref[...]
    acc = acc_ref[...]
    for i in range(npg):
        prod = (k_refs[i][...] * q[None]).reshape(page * nh, LANES).astype(BF16)
        zrep = _dot(prod, ones).reshape(page, nh, LANES)
        z = jnp.sum(jnp.where(own_lane, zrep, 0.0), axis=0) * scale + bias
        lk = -(jnp.maximum(z, 0.0) + jnp.log(1.0 + jnp.exp(-jnp.abs(z))))
        hi = lk.astype(BF16)
        lo = (lk - hi.astype(F32)).astype(BF16)
        suf = _dot(jnp.concatenate([hi, lo], axis=1), uu)
        w = jnp.exp(z + suf + run)
        wsel = jnp.where(own_lane, w[None], 0.0).reshape(page * nh, LANES).astype(BF16)
        wrep = _dot(wsel, ones).reshape(page, nh, LANES)
        acc = acc + jnp.sum(wrep * v_refs[i][...], axis=0)
        run = run + suf[:, 0:1]
    run_ref[...] = run
    acc_ref[...] = acc

    @pl.when(p == pl.num_programs(1) - 1)
    def _():
        o_ref[...] = acc


def _sb_decode(q, cache_k, cache_v, page_table, b_sb, l):
    nb = q.shape[0]
    _, _, page, nh, dh = cache_k.shape
    n_pages = page_table.shape[1]
    npg = math.gcd(DEC_PAGES, n_pages)
    assert nh == SUBLANES and dh == LANES and page == LANES
    bias = jnp.broadcast_to(b_sb[l][:, None], (nh, LANES))

    def page_spec(i):
        return pl.BlockSpec((None, None, page, nh, dh),
                            lambda b, p, pt: (l, pt[b, n_pages - 1 - (p * npg + i)], 0, 0, 0))

    grid_spec = pltpu.PrefetchScalarGridSpec(
        num_scalar_prefetch=1,
        grid=(nb, n_pages // npg),
        in_specs=[pl.BlockSpec((nh, LANES), lambda b, p, pt: (0, 0)),
                  pl.BlockSpec((None, nh, dh), lambda b, p, pt: (b, 0, 0))]
                 + [page_spec(i) for i in range(npg)] * 2,
        out_specs=pl.BlockSpec((None, nh, dh), lambda b, p, pt: (b, 0, 0)),
        scratch_shapes=[pltpu.VMEM((nh, LANES), F32), pltpu.VMEM((nh, dh), F32)],
    )
    o = pl.pallas_call(
        functools.partial(_sb_decode_kernel, page=page, npg=npg, scale=dh ** -0.5),
        grid_spec=grid_spec,
        out_shape=jax.ShapeDtypeStruct((nb, nh, dh), F32),
        compiler_params=_cparams(("parallel", "arbitrary")),
        name="sb_decode",
    )(page_table, bias, q.reshape(nb, nh, dh), *([cache_k] * npg), *([cache_v] * npg))
    return o.reshape(nb, 1, nh * dh).astype(BF16)


def _trunk(path, x, mod_all, wts, state_a, state_b, cache, dims):
    d_model = x.shape[-1]
    depth = wts["w_o"].shape[0]
    c1, c2 = dims["col_rec"], dims["col_att"]
    is_prompt = cache is None
    tm = path.tm
    sas, sbs = [], []
    kbuf = vbuf = None
    for l in range(depth):
        h = _norm_mod(path, x, wts["g_mix"], mod_all, l, 1, 0)
        y1 = _mm_nt(path, h, wts["w_in_rec"], l, c1["end"] // 2, name="in_proj_rec")
        y2, kbuf, vbuf = _in_proj_att(path, h, wts["w_in_att"], l, kbuf, vbuf, c2["kc"])
        r = _mm_nt(path, h, wts["w_in_rb"], l, LANES, name="in_proj_r")
        bv, lv, _ = y1.shape
        if is_prompt:
            o_a, s_a = _rec_a(y1, wts["lb_gamma"], wts["gn_a"], state_a[l], l, dims["chunk"])
            o_b, s_b = _rec_b(y1, r, wts["w_gk_p"], wts["b_gk"], wts["gn_b"], state_b[l], l, dims["chunk"], c1["qb"])
            o_c = _sb_prompt(y2, kbuf, vbuf, wts["b_sb"], l, c2["qc"], dims["h_c"], dims["tq"])
        else:
            f1, f2 = y1.reshape(lv, -1), y2.reshape(lv, -1)
            s1 = lambda a, b_: f1[:, c1[a]:c1[b_]]
            o_a, s_a = _dec_a(s1("qa", "fa"), s1("fa", "ia"), s1("ia", "ga"), s1("ga", "qb"),
                              wts["lb_gamma"], wts["gn_a"], state_a[l], l)
            o_b, s_b = _dec_b(s1("qb", "kb"), s1("kb", "vb"), s1("vb", "gb"), s1("gb", "end"),
                              r.reshape(lv, -1)[:, :dims["gk_rank"]], wts["w_gk_t"], wts["b_gk"], wts["gn_b"],
                              state_b[l], l)
            o_c = _sb_decode(f2, cache[0], cache[1], cache[2], wts["b_sb"], l)
            o_a, o_b, o_c = (o.reshape(bv, lv, -1) for o in (o_a, o_b, o_c))
        merged = _merge(path, h, o_a, o_b, o_c, wts["w_in_att"], c2["ma"], wts["w_br"], l, 512)
        x = _mm_resid(path, merged, wts["w_o"], x, mod_all, l, 2, 1024, name="out_proj")
        h2 = _norm_mod(path, x, wts["g_ffn"], mod_all, l, 4, 3)
        act = _swiglu(path, h2, wts["w_gu"], l, 512)
        x = _mm_resid(path, act, wts["w_down"], x, mod_all, l, 5, 512, name="down_proj")
        sas.append(s_a)
        sbs.append(s_b)
    xf = _final_norm(x, wts["g_final"], tm)
    return xf, kbuf, vbuf, jnp.stack(sas), jnp.stack(sbs)


def kernel(x_prompt, x_sample, cache_k, cache_v, state_hgrn, state_gla, page_table, c_prompt, c_sample, w_ada, b_ada, g_mix, w_in, b_sb, lb_gamma, gn_a, w_gk, b_gk, gn_b, w_br, w_o, g_ffn, w_gu, w_down, g_final):
    bp, seq, d_model = x_prompt.shape
    nb = x_sample.shape[0]
    depth, _, _, h_c, dh_c = cache_k.shape[0], None, None, cache_k.shape[3], cache_k.shape[4]
    h_a, dk_a, dv_a = state_hgrn.shape[2:]
    h_b, dk_b, dv_b = state_gla.shape[2:]
    gk_rank = w_gk.shape[1]
    w_a, w_b, w_c = h_a * dv_a, h_b * dv_b, h_c * dh_c

    def offsets(widths):
        col, off = {}, 0
        for name, wd in widths:
            col[name] = off
            off += wd
        col["end"] = off
        return col

    col_rec = offsets([("qa", h_a * dk_a), ("fa", h_a * dk_a), ("ia", w_a), ("ga", w_a),
                       ("qb", h_b * dk_b), ("kb", h_b * dk_b), ("vb", w_b), ("gb", w_b)])
    col_att = offsets([("qc", w_c), ("kc", w_c), ("vc", w_c), ("ma", d_model), ("mb", d_model), ("mc", d_model)])
    src_rb = col_rec["end"]
    w_in_t = jnp.swapaxes(w_in, 1, 2)
    w_in_rb = jnp.concatenate([w_in_t[:, src_rb:src_rb + gk_rank],
                               jnp.zeros((depth, LANES - gk_rank, d_model), w_in.dtype)], axis=1).astype(BF16)
    w_gk_p = jnp.concatenate([w_gk, jnp.zeros((depth, LANES - gk_rank, w_gk.shape[-1]), w_gk.dtype)],
                             axis=1).astype(BF16)
    wts = {
        "w_in_rec": w_in_t[:, :src_rb].astype(BF16), "w_in_att": w_in_t[:, src_rb + gk_rank:].astype(BF16),
        "w_in_rb": w_in_rb,
        "w_br": w_br.astype(BF16), "w_o": w_o.astype(BF16), "w_gu": w_gu.astype(BF16),
        "w_down": w_down.astype(BF16), "w_gk_p": w_gk_p,
        "w_gk_t": jnp.swapaxes(w_gk, 1, 2).reshape(depth, h_b, dk_b, gk_rank),
        "g_mix": g_mix, "g_ffn": g_ffn, "g_final": g_final, "b_sb": b_sb, "lb_gamma": lb_gamma,
        "gn_a": gn_a, "gn_b": gn_b, "b_gk": b_gk,
    }
    dims = {"col_rec": col_rec, "col_att": col_att, "chunk": 256, "tq": 256, "h_c": h_c, "gk_rank": gk_rank}

    c_all = jnp.concatenate([c_sample, c_prompt, jnp.zeros((16 - nb - bp, d_model), F32)], axis=0)
    mod_all = _ada_mod(c_all, w_ada, b_ada)

    prompt = _Path(bp, seq, min(1024, seq), per_row_mod=False, mod_row0=nb)
    sample = _Path(1, nb, nb, per_row_mod=True, mod_row0=0)

    zeros_a = jnp.zeros((depth, bp) + state_hgrn.shape[2:], state_hgrn.dtype)
    zeros_b = jnp.zeros((depth, bp) + state_gla.shape[2:], state_gla.dtype)
    y_p, k_p, v_p, sa_p, sb_p = _trunk(prompt, x_prompt, mod_all, wts, zeros_a, zeros_b, None, dims)

    cache = (cache_k, cache_v, page_table)
    y_s, k_s, v_s, sa_s, sb_s = _trunk(sample, x_sample.reshape(1, nb, d_model), mod_all, wts,
                                       state_hgrn, state_gla, cache, dims)

    return (y_p, y_s.reshape(nb, 1, d_model),
            k_p.reshape(depth, bp, seq, h_c, dh_c), v_p.reshape(depth, bp, seq, h_c, dh_c),
            k_s.reshape(depth, nb, 1, h_c, dh_c), v_s.reshape(depth, nb, 1, h_c, dh_c),
            sa_p, sa_s, sb_p, sb_s)
```

```python
import functools
import math

import jax
import jax.numpy as jnp
from jax import lax
from jax.experimental import pallas as pl
from jax.experimental.pallas import tpu as pltpu

F32 = jnp.float32
BF16 = jnp.bfloat16

EPS = 1e-6
GK_NORM = 16.0
N_MOD = 6
LANES = 128
SUBLANES = 8
VMEM_LIMIT = 56 * 1024 * 1024
DEC_PAGES = 16


def _cparams(sem):
    return pltpu.CompilerParams(dimension_semantics=sem, vmem_limit_bytes=VMEM_LIMIT)


def _silu(x):
    return x * jax.nn.sigmoid(x)


def _log_sigmoid(x):
    return jnp.minimum(x, 0.0) - jnp.log1p(jnp.exp(-jnp.abs(x)))


def _logaddexp(a, b):
    m = jnp.maximum(a, b)
    return m + jnp.log1p(jnp.exp(-jnp.abs(a - b)))


def _dot(a, b):
    return jnp.dot(a, b, preferred_element_type=F32)


def _dot_nt(a, b):
    return lax.dot_general(a, b, (((1,), (1,)), ((), ())), preferred_element_type=F32)


def _dot_tn(a, b):
    return lax.dot_general(a, b, (((0,), (0,)), ((), ())), preferred_element_type=F32)


class _Path:
    def __init__(self, bv, lv, tm, per_row_mod, mod_row0):
        self.bv, self.lv, self.tm = bv, lv, tm
        self.per_row_mod = per_row_mod
        self.mod_row0 = mod_row0

    def grid_rows(self):
        return (self.bv, self.lv // self.tm)

    def mod_operand(self, mod_all):
        if self.per_row_mod:
            return mod_all
        d, r, w = mod_all.shape
        return mod_all.reshape(d, r, 1, w)

    def mod_spec(self, l, k, d_model, tn=None, with_j=False):
        tn = d_model if tn is None else tn
        per = d_model // tn
        if self.per_row_mod:
            if with_j:
                return pl.BlockSpec((None, self.tm, tn), lambda b, i, j: (l, i, k * per + j))
            return pl.BlockSpec((None, self.tm, tn), lambda b, i: (l, i, k * per))
        r0 = self.mod_row0
        if with_j:
            return pl.BlockSpec((None, None, 1, tn), lambda b, i, j: (l, r0 + b, 0, k * per + j))
        return pl.BlockSpec((None, None, 1, tn), lambda b, i: (l, r0 + b, 0, k * per))


def _ada_kernel(c_ref, w_ref, b_ref, o_ref):
    a = _silu(c_ref[...]).astype(BF16)
    o_ref[...] = _dot(a, w_ref[...].astype(BF16)) + b_ref[...]


def _ada_mod(c_all, w_ada, b_ada, tn=1024):
    depth, d, n = w_ada.shape
    rows = c_all.shape[0]
    return pl.pallas_call(
        _ada_kernel,
        grid=(depth, n // tn),
        in_specs=[
            pl.BlockSpec((rows, d), lambda l, j: (0, 0)),
            pl.BlockSpec((None, d, tn), lambda l, j: (l, 0, j)),
            pl.BlockSpec((None, 1, tn), lambda l, j: (l, 0, j)),
        ],
        out_specs=pl.BlockSpec((None, rows, tn), lambda l, j: (l, 0, j)),
        out_shape=jax.ShapeDtypeStruct((depth, rows, n), F32),
        compiler_params=_cparams(("parallel", "parallel")),
        name="ada_mod",
    )(c_all, w_ada, b_ada.reshape(depth, 1, n))


def _norm_mod_kernel(x_ref, g_ref, sc_ref, sh_ref, o_ref):
    x = x_ref[...]
    y = x * lax.rsqrt(jnp.mean(x * x, axis=-1, keepdims=True) + EPS) * g_ref[...]
    o_ref[...] = (y * (1.0 + sc_ref[...]) + sh_ref[...]).astype(o_ref.dtype)


def _norm_kernel(x_ref, g_ref, o_ref):
    x = x_ref[...]
    y = x * lax.rsqrt(jnp.mean(x * x, axis=-1, keepdims=True) + EPS) * g_ref[...]
    o_ref[...] = y.astype(o_ref.dtype)


def _norm_mod(path, x, g, mod_all, l, k_sc, k_sh, tm=None):
    bv, lv, d = x.shape
    tm = path.tm if tm is None else tm
    p = _Path(bv, lv, tm, path.per_row_mod, path.mod_row0)
    depth = g.shape[0]
    mod = p.mod_operand(mod_all)
    return pl.pallas_call(
        _norm_mod_kernel,
        grid=p.grid_rows(),
        in_specs=[
            pl.BlockSpec((None, tm, d), lambda b, i: (b, i, 0)),
            pl.BlockSpec((None, 1, d), lambda b, i: (l, 0, 0)),
            p.mod_spec(l, k_sc, d),
            p.mod_spec(l, k_sh, d),
        ],
        out_specs=pl.BlockSpec((None, tm, d), lambda b, i: (b, i, 0)),
        out_shape=jax.ShapeDtypeStruct((bv, lv, d), BF16),
        compiler_params=_cparams(("parallel", "parallel")),
        name="norm_mod",
    )(x, g.reshape(depth, 1, d), mod, mod)


def _final_norm(x, g, tm):
    bv, lv, d = x.shape
    return pl.pallas_call(
        _norm_kernel,
        grid=(bv, lv // tm),
        in_specs=[
            pl.BlockSpec((None, tm, d), lambda b, i: (b, i, 0)),
            pl.BlockSpec((1, d), lambda b, i: (0, 0)),
        ],
        out_specs=pl.BlockSpec((None, tm, d), lambda b, i: (b, i, 0)),
        out_shape=jax.ShapeDtypeStruct((bv, lv, d), F32),
        compiler_params=_cparams(("parallel", "parallel")),
        name="final_norm",
    )(x, g.reshape(1, d))


def _mm_nt_kernel(a_ref, w_ref, o_ref):
    o_ref[...] = _dot_nt(a_ref[...], w_ref[...]).astype(o_ref.dtype)


def _mm_nt(path, a, w_t, l, tn, tm=None, out_dtype=F32, name="mm"):
    bv, lv, k = a.shape
    ncols = w_t.shape[1]
    tm = path.tm if tm is None else tm
    assert ncols % tn == 0 and lv % tm == 0
    return pl.pallas_call(
        _mm_nt_kernel,
        grid=(bv, lv // tm, ncols // tn),
        in_specs=[
            pl.BlockSpec((None, tm, k), lambda b, i, j: (b, i, 0)),
            pl.BlockSpec((None, tn, k), lambda b, i, j: (l, j, 0)),
        ],
        out_specs=pl.BlockSpec((None, tm, tn), lambda b, i, j: (b, i, j)),
        out_shape=jax.ShapeDtypeStruct((bv, lv, ncols), out_dtype),
        compiler_params=_cparams(("parallel", "parallel", "arbitrary")),
        name=name,
    )(a, w_t)


def _in_proj_att_kernel(*refs, aliased):
    if aliased:
        a_ref, w_ref, _, _, y_ref, k_ref, v_ref = refs
    else:
        a_ref, w_ref, y_ref, k_ref, v_ref = refs
    j = pl.program_id(2)

    @pl.when(j == 1)
    def _():
        k_ref[...] = _dot_nt(a_ref[...], w_ref[...])

    @pl.when(j == 2)
    def _():
        v_ref[...] = _dot_nt(a_ref[...], w_ref[...])

    @pl.when(j == 0)
    def _():
        y_ref[...] = _dot_nt(a_ref[...], w_ref[...])


def _in_proj_att(path, a, w_t, l, kbuf, vbuf, tn, tm=None):
    bv, lv, k = a.shape
    depth = w_t.shape[0]
    tm = path.tm if tm is None else tm
    aliased = kbuf is not None
    kv_shape = jax.ShapeDtypeStruct((depth, bv, lv, tn), F32)
    kv_spec = pl.BlockSpec((None, None, tm, tn), lambda b, i, j: (l, b, i, 0))
    any_spec = pl.BlockSpec(memory_space=pl.ANY)
    return pl.pallas_call(
        functools.partial(_in_proj_att_kernel, aliased=aliased),
        grid=(bv, lv // tm, 3),
        in_specs=[
            pl.BlockSpec((None, tm, k), lambda b, i, j: (b, i, 0)),
            pl.BlockSpec((None, tn, k), lambda b, i, j: (l, j, 0)),
        ] + ([any_spec, any_spec] if aliased else []),
        out_specs=[pl.BlockSpec((None, tm, tn), lambda b, i, j: (b, i, 0)), kv_spec, kv_spec],
        out_shape=[jax.ShapeDtypeStruct((bv, lv, tn), F32), kv_shape, kv_shape],
        input_output_aliases={2: 1, 3: 2} if aliased else {},
        compiler_params=_cparams(("parallel", "parallel", "arbitrary")),
        name="in_proj_att",
    )(a, w_t, *((kbuf, vbuf) if aliased else ()))


def _mm_resid_kernel(a_ref, w_ref, x_ref, gt_ref, o_ref):
    o_ref[...] = x_ref[...] + gt_ref[...] * _dot(a_ref[...], w_ref[...])


def _mm_resid(path, a, w, x, mod_all, l, k_gt, tn, tm=None, name="mm_resid"):
    bv, lv, k = a.shape
    d = w.shape[-1]
    tm = path.tm if tm is None else tm
    p = _Path(bv, lv, tm, path.per_row_mod, path.mod_row0)
    return pl.pallas_call(
        _mm_resid_kernel,
        grid=(bv, lv // tm, d // tn),
        in_specs=[
            pl.BlockSpec((None, tm, k), lambda b, i, j: (b, i, 0)),
            pl.BlockSpec((None, k, tn), lambda b, i, j: (l, 0, j)),
            pl.BlockSpec((None, tm, tn), lambda b, i, j: (b, i, j)),
            p.mod_spec(l, k_gt, d, tn=tn, with_j=True),
        ],
        out_specs=pl.BlockSpec((None, tm, tn), lambda b, i, j: (b, i, j)),
        out_shape=jax.ShapeDtypeStruct((bv, lv, d), F32),
        compiler_params=_cparams(("parallel", "parallel", "arbitrary")),
        name=name,
    )(a, w, x, p.mod_operand(mod_all))


def _swiglu_kernel(a_ref, wg_ref, wu_ref, o_ref):
    a = a_ref[...]
    o_ref[...] = (_silu(_dot(a, wg_ref[...])) * _dot(a, wu_ref[...])).astype(o_ref.dtype)


def _swiglu(path, a, w_gu, l, tn, tm=None):
    bv, lv, k = a.shape
    d_ff = w_gu.shape[-1] // 2
    tm = path.tm if tm is None else tm
    nj = d_ff // tn
    assert d_ff % tn == 0
    return pl.pallas_call(
        _swiglu_kernel,
        grid=(bv, lv // tm, nj),
        in_specs=[
            pl.BlockSpec((None, tm, k), lambda b, i, j: (b, i, 0)),
            pl.BlockSpec((None, k, tn), lambda b, i, j: (l, 0, j)),
            pl.BlockSpec((None, k, tn), lambda b, i, j: (l, 0, nj + j)),
        ],
        out_specs=pl.BlockSpec((None, tm, tn), lambda b, i, j: (b, i, j)),
        out_shape=jax.ShapeDtypeStruct((bv, lv, d_ff), BF16),
        compiler_params=_cparams(("parallel", "parallel", "arbitrary")),
        name="swiglu",
    )(a, w_gu, w_gu)


def _merge_kernel(h_ref, oa_ref, ob_ref, oc_ref, ga_ref, gb_ref, gc_ref, wa_ref, wb_ref, wc_ref, o_ref):
    h = h_ref[...]
    m = jax.nn.sigmoid(_dot_nt(h, ga_ref[...])) * _dot(oa_ref[...], wa_ref[...])
    m = m + jax.nn.sigmoid(_dot_nt(h, gb_ref[...])) * _dot(ob_ref[...], wb_ref[...])
    m = m + jax.nn.sigmoid(_dot_nt(h, gc_ref[...])) * _dot(oc_ref[...], wc_ref[...])
    o_ref[...] = m.astype(o_ref.dtype)


def _merge(path, h, o_a, o_b, o_c, w_t, gate_row0, w_br, l, tn, tm=None):
    bv, lv, wa = o_a.shape
    wb, wc = o_b.shape[-1], o_c.shape[-1]
    k = h.shape[-1]
    d = w_br.shape[-1]
    tm = path.tm if tm is None else tm
    g0 = gate_row0 // tn
    per = d // tn
    assert gate_row0 % tn == 0 and wb == wa and wc == 2 * wa
    return pl.pallas_call(
        _merge_kernel,
        grid=(bv, lv // tm, d // tn),
        in_specs=[
            pl.BlockSpec((None, tm, k), lambda b, i, j: (b, i, 0)),
            pl.BlockSpec((None, tm, wa), lambda b, i, j: (b, i, 0)),
            pl.BlockSpec((None, tm, wb), lambda b, i, j: (b, i, 0)),
            pl.BlockSpec((None, tm, wc), lambda b, i, j: (b, i, 0)),
            pl.BlockSpec((None, tn, k), lambda b, i, j: (l, g0 + j, 0)),
            pl.BlockSpec((None, tn, k), lambda b, i, j: (l, g0 + per + j, 0)),
            pl.BlockSpec((None, tn, k), lambda b, i, j: (l, g0 + 2 * per + j, 0)),
            pl.BlockSpec((None, wa, tn), lambda b, i, j: (l, 0, j)),
            pl.BlockSpec((None, wb, tn), lambda b, i, j: (l, 1, j)),
            pl.BlockSpec((None, wc, tn), lambda b, i, j: (l, 1, j)),
        ],
        out_specs=pl.BlockSpec((None, tm, tn), lambda b, i, j: (b, i, j)),
        out_shape=jax.ShapeDtypeStruct((bv, lv, d), BF16),
        compiler_params=_cparams(("parallel", "parallel", "arbitrary")),
        name="merge",
    )(h, o_a, o_b, o_c, w_t, w_t, w_t, w_br, w_br, w_br)


def _lb_from_gamma(gam, l):
    depth = len(gam)
    m = gam[0]
    for i in range(1, depth):
        m = jnp.maximum(m, gam[i])
    e = [jnp.exp(gam[i] - m) for i in range(depth)]
    tot = e[0]
    for i in range(1, depth):
        tot = tot + e[i]
    sm = [ei / tot for ei in e]
    cs = [sm[0]]
    for i in range(1, depth):
        cs.append(cs[-1] + sm[i])
    return cs[l] - cs[0]


def _rec_core(qq, kk, logf, v_ref_tile, g_tile, gn, st_ref, o_ref, *, chunk, n_groups, hpg):
    c = chunk
    wq = n_groups * LANES
    dk = LANES // hpg
    n_levels = int(math.log2(c))
    row = lax.broadcasted_iota(jnp.int32, (c, wq), 0)
    ti = lax.broadcasted_iota(jnp.int32, (c, c), 0)
    si = lax.broadcasted_iota(jnp.int32, (c, c), 1)
    x = jnp.bitwise_xor(ti, si)
    lvl = jnp.zeros((c, c), jnp.int32)
    for lev in range(1, n_levels + 1):
        lvl = lvl + jnp.where(x >= (1 << (lev - 1)), 1, 0)
    lvl = jnp.where(ti > si, lvl, -1)
    lane = lax.broadcasted_iota(jnp.int32, (c, LANES), 1)

    def head_mask(a, sub):
        if hpg == 1:
            return a
        return jnp.where(lane < dk, a, 0.0) if sub == 0 else jnp.where(lane >= dk, a, 0.0)

    pre = logf
    tot = logf
    ql, kl = [], []
    for lev in range(1, n_levels + 1):
        half = 1 << (lev - 1)
        upper = (row & half) != 0
        e = jnp.exp(jnp.where(upper, pre, tot - pre))
        ql.append(qq * e)
        kl.append(kk * e)
        up = pltpu.roll(tot, half, 0)
        dn = pltpu.roll(tot, c - half, 0)
        pre = pre + jnp.where(upper, up, 0.0)
        tot = tot + jnp.where(upper, up, dn)
    q_in = qq * jnp.exp(pre)
    k_out = kk * jnp.exp(tot - pre)
    d_all = jnp.exp(tot[0:1, :])
    qk = qq * kk

    for g in range(n_groups):
        gs = slice(g * LANES, (g + 1) * LANES)
        st = st_ref[g]
        st_bf = st.astype(BF16)
        st_new = st * d_all[:, gs]
        for sub in range(hpg):
            h = g * hpg + sub
            hs = slice(h * LANES, (h + 1) * LANES)
            v_bf = v_ref_tile[:, hs].astype(BF16)
            att = jnp.zeros((c, c), F32)
            for lev in range(1, n_levels + 1):
                p = _dot_nt(head_mask(ql[lev - 1][:, gs], sub).astype(BF16), kl[lev - 1][:, gs].astype(BF16))
                att = jnp.where(lvl == lev, p, att)
            diag = jnp.sum(head_mask(qk[:, gs], sub), axis=-1, keepdims=True)
            att = jnp.where(x == 0, diag, att)
            o = _dot_nt(head_mask(q_in[:, gs], sub).astype(BF16), st_bf) + _dot(att.astype(BF16), v_bf)
            st_new = st_new + _dot_tn(v_bf, head_mask(k_out[:, gs], sub).astype(BF16))
            y = o * lax.rsqrt(jnp.mean(o * o, axis=-1, keepdims=True) + EPS) * gn
            o_ref[:, hs] = (y * _silu(g_tile[:, hs])).astype(o_ref.dtype)
        st_ref[g] = st_new


def _state_in(s0_ref, st_ref, n_groups, hpg):
    dk = LANES // hpg
    for g in range(n_groups):
        blk = jnp.concatenate([s0_ref[g * hpg + sub] for sub in range(hpg)], axis=0) if hpg > 1 else s0_ref[g]
        st_ref[g] = blk.T


def _state_out(st_ref, sout_ref, n_groups, hpg):
    dk = LANES // hpg
    for g in range(n_groups):
        t = st_ref[g].T
        for sub in range(hpg):
            sout_ref[g * hpg + sub] = t[sub * dk:(sub + 1) * dk, :]


def _rec_a_kernel(q_ref, f_ref, i_ref, g_ref, lbg_ref, gn_ref, s0_ref, o_ref, sout_ref, st_ref, *, l, chunk):
    ci = pl.program_id(1)

    @pl.when(ci == 0)
    def _():
        _state_in(s0_ref, st_ref, 4, 1)

    lb = _lb_from_gamma([lbg_ref[i:i + 1, :] for i in range(lbg_ref.shape[0])], l)
    f = f_ref[...]
    logf = _logaddexp(jnp.log(lb), jnp.log1p(-lb) + _log_sigmoid(f))
    kk = (1.0 - lb) * jax.nn.sigmoid(-f)
    qq = _silu(q_ref[...]) * (LANES ** -0.5)
    _rec_core(qq, kk, logf, i_ref, g_ref[...], gn_ref[...], st_ref, o_ref, chunk=chunk, n_groups=4, hpg=1)

    @pl.when(ci == pl.num_programs(1) - 1)
    def _():
        _state_out(st_ref, sout_ref, 4, 1)


def _rec_b_kernel(q_ref, k_ref, v_ref, g_ref, r_ref, wgk_ref, bgk_ref, gn_ref, s0_ref, o_ref, sout_ref, st_ref,
                  *, chunk):
    ci = pl.program_id(1)

    @pl.when(ci == 0)
    def _():
        _state_in(s0_ref, st_ref, 2, 2)

    gk = _dot(r_ref[...].astype(BF16), wgk_ref[...]) + bgk_ref[...]
    logf = _log_sigmoid(gk) * (1.0 / GK_NORM)
    qq = q_ref[...] * ((LANES // 2) ** -0.5)
    _rec_core(qq, k_ref[...], logf, v_ref, g_ref[...], gn_ref[...], st_ref, o_ref, chunk=chunk, n_groups=2, hpg=2)

    @pl.when(ci == pl.num_programs(1) - 1)
    def _():
        _state_out(st_ref, sout_ref, 2, 2)


def _rec_a(y, lb_gamma, gn_a, s0, l, chunk):
    b, L, _ = y.shape
    depth = lb_gamma.shape[0]
    w = 4 * LANES
    col = lambda k: pl.BlockSpec((None, chunk, w), lambda bi, ci: (bi, ci, k))
    return pl.pallas_call(
        functools.partial(_rec_a_kernel, l=l, chunk=chunk),
        grid=(b, L // chunk),
        in_specs=[
            col(0), col(1), col(2), col(3),
            pl.BlockSpec((depth, w), lambda bi, ci: (0, 0)),
            pl.BlockSpec((None, 1, LANES), lambda bi, ci: (l, 0, 0)),
            pl.BlockSpec((None, 4, LANES, LANES), lambda bi, ci: (bi, 0, 0, 0)),
        ],
        out_specs=[
            pl.BlockSpec((None, chunk, w), lambda bi, ci: (bi, ci, 0)),
            pl.BlockSpec((None, 4, LANES, LANES), lambda bi, ci: (bi, 0, 0, 0)),
        ],
        out_shape=[jax.ShapeDtypeStruct((b, L, w), BF16), jax.ShapeDtypeStruct(s0.shape, F32)],
        scratch_shapes=[pltpu.VMEM((4, LANES, LANES), F32)],
        compiler_params=_cparams(("parallel", "arbitrary")),
        name="rec_hgrn",
    )(y, y, y, y, lb_gamma, gn_a.reshape(depth, 1, LANES), s0)


def _rec_b(y, r, wgk_p, b_gk, gn_b, s0, l, chunk, col_b):
    b, L, _ = y.shape
    depth = gn_b.shape[0]
    c256 = col_b // 256
    c512 = (col_b + 512) // 512
    return pl.pallas_call(
        functools.partial(_rec_b_kernel, chunk=chunk),
        grid=(b, L // chunk),
        in_specs=[
            pl.BlockSpec((None, chunk, 256), lambda bi, ci: (bi, ci, c256)),
            pl.BlockSpec((None, chunk, 256), lambda bi, ci: (bi, ci, c256 + 1)),
            pl.BlockSpec((None, chunk, 512), lambda bi, ci: (bi, ci, c512)),
            pl.BlockSpec((None, chunk, 512), lambda bi, ci: (bi, ci, c512 + 1)),
            pl.BlockSpec((None, chunk, LANES), lambda bi, ci: (bi, ci, 0)),
            pl.BlockSpec((None, LANES, 256), lambda bi, ci: (l, 0, 0)),
            pl.BlockSpec((None, 1, 256), lambda bi, ci: (l, 0, 0)),
            pl.BlockSpec((None, 1, LANES), lambda bi, ci: (l, 0, 0)),
            pl.BlockSpec((None, 4, 64, LANES), lambda bi, ci: (bi, 0, 0, 0)),
        ],
        out_specs=[
            pl.BlockSpec((None, chunk, 512), lambda bi, ci: (bi, ci, 0)),
            pl.BlockSpec((None, 4, 64, LANES), lambda bi, ci: (bi, 0, 0, 0)),
        ],
        out_shape=[jax.ShapeDtypeStruct((b, L, 512), BF16), jax.ShapeDtypeStruct(s0.shape, F32)],
        scratch_shapes=[pltpu.VMEM((2, LANES, LANES), F32)],
        compiler_params=_cparams(("parallel", "arbitrary")),
        name="rec_gla",
    )(y, y, y, y, r, wgk_p, b_gk.reshape(depth, 1, 256), gn_b.reshape(depth, 1, LANES), s0)


def _dec_finish(s_new, q_col, g_row, gn, o_ref, h):
    o = jnp.sum(q_col * s_new, axis=0, keepdims=True)
    y = o * lax.rsqrt(jnp.mean(o * o, axis=-1, keepdims=True) + EPS) * gn
    o_ref[h] = y * _silu(g_row)


def _dec_a_kernel(q_ref, f_ref, i_ref, g_ref, lbg_ref, gn_ref, s0_ref, o_ref, sout_ref, *, l):
    lb_all = _lb_from_gamma([lbg_ref[i] for i in range(lbg_ref.shape[0])], l)
    for h in range(4):
        lb = lb_all[h]
        f = f_ref[h]
        logf = _logaddexp(jnp.log(lb), jnp.log1p(-lb) + _log_sigmoid(f))
        kk = (1.0 - lb) * jax.nn.sigmoid(-f)
        qq = _silu(q_ref[h]) * (LANES ** -0.5)
        s_new = jnp.exp(logf) * s0_ref[h] + kk * i_ref[h]
        sout_ref[h] = s_new
        _dec_finish(s_new, qq, g_ref[h], gn_ref[...], o_ref, h)


def _dec_b_kernel(q_ref, k_ref, v_ref, g_ref, r_ref, wgk_ref, bgk_ref, gn_ref, s0_ref, o_ref, sout_ref):
    r = r_ref[...]
    for h in range(4):
        gk = jnp.sum(wgk_ref[h] * r, axis=-1, keepdims=True) + bgk_ref[h]
        logf = _log_sigmoid(gk) * (1.0 / GK_NORM)
        qq = q_ref[h] * ((LANES // 2) ** -0.5)
        s_new = jnp.exp(logf) * s0_ref[h] + k_ref[h] * v_ref[h]
        sout_ref[h] = s_new
        _dec_finish(s_new, qq, g_ref[h], gn_ref[...], o_ref, h)


def _dec_a(qa, fa, ia, ga, lb_gamma, gn_a, s0, l):
    nb = qa.shape[0]
    depth = lb_gamma.shape[0]
    colspec = pl.BlockSpec((None, 4, LANES, 1), lambda b: (b, 0, 0, 0))
    rowspec = pl.BlockSpec((None, 4, 1, LANES), lambda b: (b, 0, 0, 0))
    stspec = pl.BlockSpec((None, 4, LANES, LANES), lambda b: (b, 0, 0, 0))
    o, s = pl.pallas_call(
        functools.partial(_dec_a_kernel, l=l),
        grid=(nb,),
        in_specs=[colspec, colspec, rowspec, rowspec,
                  pl.BlockSpec((depth, 4, LANES, 1), lambda b: (0, 0, 0, 0)),
                  pl.BlockSpec((None, 1, LANES), lambda b: (l, 0, 0)),
                  stspec],
        out_specs=[rowspec, stspec],
        out_shape=[jax.ShapeDtypeStruct((nb, 4, 1, LANES), F32), jax.ShapeDtypeStruct(s0.shape, F32)],
        compiler_params=_cparams(("parallel",)),
        name="dec_hgrn",
    )(qa.reshape(nb, 4, LANES, 1), fa.reshape(nb, 4, LANES, 1), ia.reshape(nb, 4, 1, LANES),
      ga.reshape(nb, 4, 1, LANES), lb_gamma.reshape(depth, 4, LANES, 1), gn_a.reshape(depth, 1, LANES), s0)
    return o.reshape(nb, 1, 4 * LANES).astype(BF16), s


def _dec_b(qb, kb, vb, gb, rb, wgk_t, b_gk, gn_b, s0, l):
    nb = qb.shape[0]
    depth = gn_b.shape[0]
    rank = rb.shape[-1]
    colspec = pl.BlockSpec((None, 4, 64, 1), lambda b: (b, 0, 0, 0))
    rowspec = pl.BlockSpec((None, 4, 1, LANES), lambda b: (b, 0, 0, 0))
    stspec = pl.BlockSpec((None, 4, 64, LANES), lambda b: (b, 0, 0, 0))
    o, s = pl.pallas_call(
        _dec_b_kernel,
        grid=(nb,),
        in_specs=[colspec, colspec, rowspec, rowspec,
                  pl.BlockSpec((None, 1, rank), lambda b: (b, 0, 0)),
                  pl.BlockSpec((None, 4, 64, rank), lambda b: (l, 0, 0, 0)),
                  pl.BlockSpec((None, 4, 64, 1), lambda b: (l, 0, 0, 0)),
                  pl.BlockSpec((None, 1, LANES), lambda b: (l, 0, 0)),
                  stspec],
        out_specs=[rowspec, stspec],
        out_shape=[jax.ShapeDtypeStruct((nb, 4, 1, LANES), F32), jax.ShapeDtypeStruct(s0.shape, F32)],
        compiler_params=_cparams(("parallel",)),
        name="dec_gla",
    )(qb.reshape(nb, 4, 64, 1), kb.reshape(nb, 4, 64, 1), vb.reshape(nb, 4, 1, LANES),
      gb.reshape(nb, 4, 1, LANES), rb.reshape(nb, 1, rank), wgk_t, b_gk.reshape(depth, 4, 64, 1),
      gn_b.reshape(depth, 1, LANES), s0)
    return o.reshape(nb, 1, 4 * LANES).astype(BF16), s


def _suffix_matrix(n, rows_first):
    a = lax.broadcasted_iota(jnp.int32, (n, n), 0)
    b = lax.broadcasted_iota(jnp.int32, (n, n), 1)
    if rows_first:
        m = jnp.where(a >= b, 1.0, 0.0).astype(BF16)
        return jnp.concatenate([m, m], axis=0)
    m = jnp.where(b >= a, 1.0, 0.0).astype(BF16)
    return jnp.concatenate([m, m], axis=1)


def _sb_prompt_kernel(bias_ref, qa_ref, qb_ref, k_ref, v_ref, oa_ref, ob_ref,
                      kbf_ref, vbf_ref, uu_ref, z_ref, sp_ref, suf_ref, *, l, tq, nq, scale, unroll):
    h = pl.program_id(1)
    i = pl.program_id(2)
    n_off = nq - 1

    @pl.when(i == 0)
    def _():
        kbf_ref[...] = k_ref[...].astype(BF16)
        vbf_ref[...] = v_ref[...].astype(BF16)
        uu_ref[...] = _suffix_matrix(tq, rows_first=True)[0:tq]

    bias = bias_ref[l, h]
    qa = qa_ref[...].astype(BF16)
    qb = qb_ref[...].astype(BF16)
    uu = uu_ref[...]
    ti = lax.broadcasted_iota(jnp.int32, (tq, tq), 0)
    si = lax.broadcasted_iota(jnp.int32, (tq, tq), 1)
    causal = si < ti

    def off_kb(s):
        return jnp.where(s < i, i - 1 - s, nq - 2 - s)

    def logits(slot, q, kb, masked):
        start = pl.multiple_of(kb * tq, tq)
        z = _dot_nt(q, kbf_ref[pl.ds(start, tq), :]) * scale + bias
        neg_abs = lax.bitcast_convert_type(
            lax.bitcast_convert_type(z, jnp.uint32) | jnp.uint32(0x80000000), F32)
        sp = jnp.maximum(z, 0.0) + jnp.log(1.0 + jnp.exp(neg_abs))
        if masked:
            sp = jnp.where(causal, sp, 0.0)
        z_ref[slot] = z
        sp_ref[slot] = sp.astype(BF16)

    def suffix(slot):
        suf_ref[slot] = _dot(sp_ref[slot], uu)

    def weigh(slot, kb, run, acc, diagonal):
        start = pl.multiple_of(kb * tq, tq)
        suf = suf_ref[slot]
        if diagonal:
            w = jnp.where(causal, jnp.exp(z_ref[slot] - suf), 0.0)
            return suf[:, 0:1], _dot(w.astype(BF16), vbf_ref[pl.ds(start, tq), :])
        w = jnp.exp(z_ref[slot] - suf - run)
        return run + suf[:, 0:1], acc + _dot(w.astype(BF16), vbf_ref[pl.ds(start, tq), :])

    def sweep(diagonals, fn):
        diagonals()
        for u in range(unroll):
            fn(u)

        def body(j, c):
            for u in range(unroll):
                fn(unroll * j + u)
            return c
        lax.fori_loop(1, n_off // unroll, body, 0)

    sweep(lambda: (logits(0, qa, i, True), logits(1, qb, nq - 1 - i, True)),
          lambda s: logits(2 + s, jnp.where(s < i, qa, qb), off_kb(s), False))
    sweep(lambda: (suffix(0), suffix(1)), lambda s: suffix(2 + s))

    run_a, acc_a = weigh(0, i, None, None, True)
    run_b, acc_b = weigh(1, nq - 1 - i, None, None, True)

    def chain(s, c):
        run, acc, out_a = c
        at_b = s == i
        out_a = jnp.where(at_b, acc, out_a)
        run = jnp.where(at_b, run_b, run)
        acc = jnp.where(at_b, acc_b, acc)
        run, acc = weigh(2 + s, off_kb(s), run, acc, False)
        return run, acc, out_a

    def body3(j, c):
        for u in range(unroll):
            c = chain(unroll * j + u, c)
        return c

    c = (run_a, acc_a, acc_a)
    for u in range(unroll):
        c = chain(u, c)
    _, acc, out_a = lax.fori_loop(1, n_off // unroll, body3, c)
    oa_ref[...] = out_a.astype(oa_ref.dtype)
    ob_ref[...] = acc.astype(ob_ref.dtype)


def _sb_prompt(y, kbuf, vbuf, b_sb, l, col_q, n_heads, tq):
    b, L, _ = y.shape
    cq = col_q // LANES
    nq = L // tq
    n_off = nq - 1
    assert nq % 2 == 0
    unroll = max(u for u in (1, 2, 3, 4, 5, 15) if n_off % u == 0)
    half = nq // 2
    n_slots = nq + 1
    o_lo, o_hi = pl.pallas_call(
        functools.partial(_sb_prompt_kernel, l=l, tq=tq, nq=nq, scale=LANES ** -0.5, unroll=unroll),
        grid=(b, n_heads, half),
        in_specs=[
            pl.BlockSpec(memory_space=pltpu.SMEM),
            pl.BlockSpec((None, tq, LANES), lambda bi, h, i: (bi, i, cq + h)),
            pl.BlockSpec((None, tq, LANES), lambda bi, h, i: (bi, nq - 1 - i, cq + h)),
            pl.BlockSpec((None, None, L, LANES), lambda bi, h, i: (l, bi, 0, h)),
            pl.BlockSpec((None, None, L, LANES), lambda bi, h, i: (l, bi, 0, h)),
        ],
        out_specs=[pl.BlockSpec((None, tq, LANES), lambda bi, h, i: (bi, i, h)),
                   pl.BlockSpec((None, tq, LANES), lambda bi, h, i: (bi, half - 1 - i, h))],
        out_shape=[jax.ShapeDtypeStruct((b, L // 2, n_heads * LANES), BF16)] * 2,
        scratch_shapes=[pltpu.VMEM((L, LANES), BF16), pltpu.VMEM((L, LANES), BF16),
                        pltpu.VMEM((tq, tq), BF16),
                        pltpu.VMEM((n_slots, tq, tq), F32), pltpu.VMEM((n_slots, tq, tq), BF16),
                        pltpu.VMEM((n_slots, tq, tq), F32)],
        compiler_params=_cparams(("parallel", "parallel", "arbitrary")),
        name="sb_prompt",
    )(b_sb, y, y, kbuf, vbuf)
    return jnp.concatenate([o_lo, o_hi], axis=1)


def _sb_decode_kernel(pt_ref, bias_ref, q_ref, *refs, page, npg, scale):
    k_refs, v_refs = refs[:npg], refs[npg:2 * npg]
    o_ref, run_ref, acc_ref = refs[2 * npg:]
    p = pl.program_id(1)

    @pl.when(p == 0)
    def _():
        run_ref[...] = jnp.zeros_like(run_ref)
        acc_ref[...] = jnp.zeros_like(acc_ref)

    q = q_ref[...]
    bias = bias_ref[...]
    nh = q.shape[0]
    tok = lax.broadcasted_iota(jnp.int32, (page, nh, LANES), 0)
    lane = lax.broadcasted_iota(jnp.int32, (page, nh, LANES), 2)
    own_lane = tok == lane
    ones = jnp.ones((LANES, LANES), BF16)
    uu = _suffix_matrix(page, rows_first=True)
    run = run_ref[...]
    acc = acc_ref[...]
    for i in range(npg):
        prod = (k_refs[i][...] * q[None]).reshape(page * nh, LANES).astype(BF16)
        zrep = _dot(prod, ones).reshape(page, nh, LANES)
        z = jnp.sum(jnp.where(own_lane, zrep, 0.0), axis=0) * scale + bias
        lk = -(jnp.maximum(z, 0.0) + jnp.log(1.0 + jnp.exp(-jnp.abs(z))))
        hi = lk.astype(BF16)
        lo = (lk - hi.astype(F32)).astype(BF16)
        suf = _dot(jnp.concatenate([hi, lo], axis=1), uu)
        w = jnp.exp(z + suf + run)
        wsel = jnp.where(own_lane, w[None], 0.0).reshape(page * nh, LANES).astype(BF16)
        wrep = _dot(wsel, ones).reshape(page, nh, LANES)
        acc = acc + jnp.sum(wrep * v_refs[i][...], axis=0)
        run = run + suf[:, 0:1]
    run_ref[...] = run
    acc_ref[...] = acc

    @pl.when(p == pl.num_programs(1) - 1)
    def _():
        o_ref[...] = acc


def _sb_decode(q, cache_k, cache_v, page_table, b_sb, l):
    nb = q.shape[0]
    _, _, page, nh, dh = cache_k.shape
    n_pages = page_table.shape[1]
    npg = math.gcd(DEC_PAGES, n_pages)
    assert nh == SUBLANES and dh == LANES and page == LANES
    bias = jnp.broadcast_to(b_sb[l][:, None], (nh, LANES))

    def page_spec(i):
        return pl.BlockSpec((None, None, page, nh, dh),
                            lambda b, p, pt: (l, pt[b, n_pages - 1 - (p * npg + i)], 0, 0, 0))

    grid_spec = pltpu.PrefetchScalarGridSpec(
        num_scalar_prefetch=1,
        grid=(nb, n_pages // npg),
        in_specs=[pl.BlockSpec((nh, LANES), lambda b, p, pt: (0, 0)),
                  pl.BlockSpec((None, nh, dh), lambda b, p, pt: (b, 0, 0))]
                 + [page_spec(i) for i in range(npg)] * 2,
        out_specs=pl.BlockSpec((None, nh, dh), lambda b, p, pt: (b, 0, 0)),
        scratch_shapes=[pltpu.VMEM((nh, LANES), F32), pltpu.VMEM((nh, dh), F32)],
    )
    o = pl.pallas_call(
        functools.partial(_sb_decode_kernel, page=page, npg=npg, scale=dh ** -0.5),
        grid_spec=grid_spec,
        out_shape=jax.ShapeDtypeStruct((nb, nh, dh), F32),
        compiler_params=_cparams(("parallel", "arbitrary")),
        name="sb_decode",
    )(page_table, bias, q.reshape(nb, nh, dh), *([cache_k] * npg), *([cache_v] * npg))
    return o.reshape(nb, 1, nh * dh).astype(BF16)


def _trunk(path, x, mod_all, wts, state_a, state_b, cache, dims):
    d_model = x.shape[-1]
    depth = wts["w_o"].shape[0]
    c1, c2 = dims["col_rec"], dims["col_att"]
    is_prompt = cache is None
    tm = path.tm
    sas, sbs = [], []
    kbuf = vbuf = None
    for l in range(depth):
        h = _norm_mod(path, x, wts["g_mix"], mod_all, l, 1, 0)
        y1 = _mm_nt(path, h, wts["w_in_rec"], l, c1["end"] // 2, name="in_proj_rec")
        y2, kbuf, vbuf = _in_proj_att(path, h, wts["w_in_att"], l, kbuf, vbuf, c2["kc"])
        r = _mm_nt(path, h, wts["w_in_rb"], l, LANES, name="in_proj_r")
        bv, lv, _ = y1.shape
        if is_prompt:
            o_a, s_a = _rec_a(y1, wts["lb_gamma"], wts["gn_a"], state_a[l], l, dims["chunk"])
            o_b, s_b = _rec_b(y1, r, wts["w_gk_p"], wts["b_gk"], wts["gn_b"], state_b[l], l, dims["chunk"], c1["qb"])
            o_c = _sb_prompt(y2, kbuf, vbuf, wts["b_sb"], l, c2["qc"], dims["h_c"], dims["tq"])
        else:
            f1, f2 = y1.reshape(lv, -1), y2.reshape(lv, -1)
            s1 = lambda a, b_: f1[:, c1[a]:c1[b_]]
            o_a, s_a = _dec_a(s1("qa", "fa"), s1("fa", "ia"), s1("ia", "ga"), s1("ga", "qb"),
                              wts["lb_gamma"], wts["gn_a"], state_a[l], l)
            o_b, s_b = _dec_b(s1("qb", "kb"), s1("kb", "vb"), s1("vb", "gb"), s1("gb", "end"),
                              r.reshape(lv, -1)[:, :dims["gk_rank"]], wts["w_gk_t"], wts["b_gk"], wts["gn_b"],
                              state_b[l], l)
            o_c = _sb_decode(f2, cache[0], cache[1], cache[2], wts["b_sb"], l)
            o_a, o_b, o_c = (o.reshape(bv, lv, -1) for o in (o_a, o_b, o_c))
        merged = _merge(path, h, o_a, o_b, o_c, wts["w_in_att"], c2["ma"], wts["w_br"], l, 512)
        x = _mm_resid(path, merged, wts["w_o"], x, mod_all, l, 2, 1024, name="out_proj")
        h2 = _norm_mod(path, x, wts["g_ffn"], mod_all, l, 4, 3)
        act = _swiglu(path, h2, wts["w_gu"], l, 512)
        x = _mm_resid(path, act, wts["w_down"], x, mod_all, l, 5, 512, name="down_proj")
        sas.append(s_a)
        sbs.append(s_b)
    xf = _final_norm(x, wts["g_final"], tm)
    return xf, kbuf, vbuf, jnp.stack(sas), jnp.stack(sbs)


def kernel(x_prompt, x_sample, cache_k, cache_v, state_hgrn, state_gla, page_table, c_prompt, c_sample, w_ada, b_ada, g_mix, w_in, b_sb, lb_gamma, gn_a, w_gk, b_gk, gn_b, w_br, w_o, g_ffn, w_gu, w_down, g_final):
    bp, seq, d_model = x_prompt.shape
    nb = x_sample.shape[0]
    depth, _, _, h_c, dh_c = cache_k.shape[0], None, None, cache_k.shape[3], cache_k.shape[4]
    h_a, dk_a, dv_a = state_hgrn.shape[2:]
    h_b, dk_b, dv_b = state_gla.shape[2:]
    gk_rank = w_gk.shape[1]
    w_a, w_b, w_c = h_a * dv_a, h_b * dv_b, h_c * dh_c

    def offsets(widths):
        col, off = {}, 0
        for name, wd in widths:
            col[name] = off
            off += wd
        col["end"] = off
        return col

    col_rec = offsets([("qa", h_a * dk_a), ("fa", h_a * dk_a), ("ia", w_a), ("ga", w_a),
                       ("qb", h_b * dk_b), ("kb", h_b * dk_b), ("vb", w_b), ("gb", w_b)])
    col_att = offsets([("qc", w_c), ("kc", w_c), ("vc", w_c), ("ma", d_model), ("mb", d_model), ("mc", d_model)])
    src_rb = col_rec["end"]
    w_in_t = jnp.swapaxes(w_in, 1, 2)
    w_in_rb = jnp.concatenate([w_in_t[:, src_rb:src_rb + gk_rank],
                               jnp.zeros((depth, LANES - gk_rank, d_model), w_in.dtype)], axis=1).astype(BF16)
    w_gk_p = jnp.concatenate([w_gk, jnp.zeros((depth, LANES - gk_rank, w_gk.shape[-1]), w_gk.dtype)],
                             axis=1).astype(BF16)
    wts = {
        "w_in_rec": w_in_t[:, :src_rb].astype(BF16), "w_in_att": w_in_t[:, src_rb + gk_rank:].astype(BF16),
        "w_in_rb": w_in_rb,
        "w_br": w_br.astype(BF16), "w_o": w_o.astype(BF16), "w_gu": w_gu.astype(BF16),
        "w_down": w_down.astype(BF16), "w_gk_p": w_gk_p,
        "w_gk_t": jnp.swapaxes(w_gk, 1, 2).reshape(depth, h_b, dk_b, gk_rank),
        "g_mix": g_mix, "g_ffn": g_ffn, "g_final": g_final, "b_sb": b_sb, "lb_gamma": lb_gamma,
        "gn_a": gn_a, "gn_b": gn_b, "b_gk": b_gk,
    }
    dims = {"col_rec": col_rec, "col_att": col_att, "chunk": 256, "tq": 256, "h_c": h_c, "gk_rank": gk_rank}

    c_all = jnp.concatenate([c_sample, c_prompt, jnp.zeros((16 - nb - bp, d_model), F32)], axis=0)
    mod_all = _ada_mod(c_all, w_ada, b_ada)

    prompt = _Path(bp, seq, min(1024, seq), per_row_mod=False, mod_row0=nb)
    sample = _Path(1, nb, nb, per_row_mod=True, mod_row0=0)

    zeros_a = jnp.zeros((depth, bp) + state_hgrn.shape[2:], state_hgrn.dtype)
    zeros_b = jnp.zeros((depth, bp) + state_gla.shape[2:], state_gla.dtype)
    y_p, k_p, v_p, sa_p, sb_p = _trunk(prompt, x_prompt, mod_all, wts, zeros_a, zeros_b, None, dims)

    cache = (cache_k, cache_v, page_table)
    y_s, k_s, v_s, sa_s, sb_s = _trunk(sample, x_sample.reshape(1, nb, d_model), mod_all, wts,
                                       state_hgrn, state_gla, cache, dims)

    return (y_p, y_s.reshape(nb, 1, d_model),
            k_p.reshape(depth, bp, seq, h_c, dh_c), v_p.reshape(depth, bp, seq, h_c, dh_c),
            k_s.reshape(depth, nb, 1, h_c, dh_c), v_s.reshape(depth, nb, 1, h_c, dh_c),
            sa_p, sa_s, sb_p, sb_s)
```

```python
import functools
import math

import jax
import jax.numpy as jnp
from jax import lax
from jax.experimental import pallas as pl
from jax.experimental.pallas import tpu as pltpu

F32 = jnp.float32
BF16 = jnp.bfloat16

EPS = 1e-6
GK_NORM = 16.0
N_MOD = 6
LANES = 128
SUBLANES = 8
VMEM_LIMIT = 56 * 1024 * 1024
DEC_PAGES = 16


def _cparams(sem):
    return pltpu.CompilerParams(dimension_semantics=sem, vmem_limit_bytes=VMEM_LIMIT)


def _silu(x):
    return x * jax.nn.sigmoid(x)


def _log_sigmoid(x):
    return jnp.minimum(x, 0.0) - jnp.log1p(jnp.exp(-jnp.abs(x)))


def _logaddexp(a, b):
    m = jnp.maximum(a, b)
    return m + jnp.log1p(jnp.exp(-jnp.abs(a - b)))


def _dot(a, b):
    return jnp.dot(a, b, preferred_element_type=F32)


def _dot_nt(a, b):
    return lax.dot_general(a, b, (((1,), (1,)), ((), ())), preferred_element_type=F32)


def _dot_tn(a, b):
    return lax.dot_general(a, b, (((0,), (0,)), ((), ())), preferred_element_type=F32)


class _Path:
    def __init__(self, bv, lv, tm, per_row_mod, mod_row0):
        self.bv, self.lv, self.tm = bv, lv, tm
        self.per_row_mod = per_row_mod
        self.mod_row0 = mod_row0

    def grid_rows(self):
        return (self.bv, self.lv // self.tm)

    def mod_operand(self, mod_all):
        if self.per_row_mod:
            return mod_all
        d, r, w = mod_all.shape
        return mod_all.reshape(d, r, 1, w)

    def mod_spec(self, l, k, d_model, tn=None, with_j=False):
        tn = d_model if tn is None else tn
        per = d_model // tn
        if self.per_row_mod:
            if with_j:
                return pl.BlockSpec((None, self.tm, tn), lambda b, i, j: (l, i, k * per + j))
            return pl.BlockSpec((None, self.tm, tn), lambda b, i: (l, i, k * per))
        r0 = self.mod_row0
        if with_j:
            return pl.BlockSpec((None, None, 1, tn), lambda b, i, j: (l, r0 + b, 0, k * per + j))
        return pl.BlockSpec((None, None, 1, tn), lambda b, i: (l, r0 + b, 0, k * per))


def _ada_kernel(c_ref, w_ref, b_ref, o_ref):
    a = _silu(c_ref[...]).astype(BF16)
    o_ref[...] = _dot(a, w_ref[...].astype(BF16)) + b_ref[...]


def _ada_mod(c_all, w_ada, b_ada, tn=1024):
    depth, d, n = w_ada.shape
    rows = c_all.shape[0]
    return pl.pallas_call(
        _ada_kernel,
        grid=(depth, n // tn),
        in_specs=[
            pl.BlockSpec((rows, d), lambda l, j: (0, 0)),
            pl.BlockSpec((None, d, tn), lambda l, j: (l, 0, j)),
            pl.BlockSpec((None, 1, tn), lambda l, j: (l, 0, j)),
        ],
        out_specs=pl.BlockSpec((None, rows, tn), lambda l, j: (l, 0, j)),
        out_shape=jax.ShapeDtypeStruct((depth, rows, n), F32),
        compiler_params=_cparams(("parallel", "parallel")),
        name="ada_mod",
    )(c_all, w_ada, b_ada.reshape(depth, 1, n))


def _norm_mod_kernel(x_ref, g_ref, sc_ref, sh_ref, o_ref):
    x = x_ref[...]
    y = x * lax.rsqrt(jnp.mean(x * x, axis=-1, keepdims=True) + EPS) * g_ref[...]
    o_ref[...] = (y * (1.0 + sc_ref[...]) + sh_ref[...]).astype(o_ref.dtype)


def _norm_kernel(x_ref, g_ref, o_ref):
    x = x_ref[...]
    y = x * lax.rsqrt(jnp.mean(x * x, axis=-1, keepdims=True) + EPS) * g_ref[...]
    o_ref[...] = y.astype(o_ref.dtype)


def _norm_mod(path, x, g, mod_all, l, k_sc, k_sh, tm=None):
    bv, lv, d = x.shape
    tm = path.tm if tm is None else tm
    p = _Path(bv, lv, tm, path.per_row_mod, path.mod_row0)
    depth = g.shape[0]
    mod = p.mod_operand(mod_all)
    return pl.pallas_call(
        _norm_mod_kernel,
        grid=p.grid_rows(),
        in_specs=[
            pl.BlockSpec((None, tm, d), lambda b, i: (b, i, 0)),
            pl.BlockSpec((None, 1, d), lambda b, i: (l, 0, 0)),
            p.mod_spec(l, k_sc, d),
            p.mod_spec(l, k_sh, d),
        ],
        out_specs=pl.BlockSpec((None, tm, d), lambda b, i: (b, i, 0)),
        out_shape=jax.ShapeDtypeStruct((bv, lv, d), BF16),
        compiler_params=_cparams(("parallel", "parallel")),
        name="norm_mod",
    )(x, g.reshape(depth, 1, d), mod, mod)


def _final_norm(x, g, tm):
    bv, lv, d = x.shape
    return pl.pallas_call(
        _norm_kernel,
        grid=(bv, lv // tm),
        in_specs=[
            pl.BlockSpec((None, tm, d), lambda b, i: (b, i, 0)),
            pl.BlockSpec((1, d), lambda b, i: (0, 0)),
        ],
        out_specs=pl.BlockSpec((None, tm, d), lambda b, i: (b, i, 0)),
        out_shape=jax.ShapeDtypeStruct((bv, lv, d), F32),
        compiler_params=_cparams(("parallel", "parallel")),
        name="final_norm",
    )(x, g.reshape(1, d))


def _mm_nt_kernel(a_ref, w_ref, o_ref):
    o_ref[...] = _dot_nt(a_ref[...], w_ref[...]).astype(o_ref.dtype)


def _mm_nt(path, a, w_t, l, tn, tm=None, out_dtype=F32, name="mm"):
    bv, lv, k = a.shape
    ncols = w_t.shape[1]
    tm = path.tm if tm is None else tm
    assert ncols % tn == 0 and lv % tm == 0
    return pl.pallas_call(
        _mm_nt_kernel,
        grid=(bv, lv // tm, ncols // tn),
        in_specs=[
            pl.BlockSpec((None, tm, k), lambda b, i, j: (b, i, 0)),
            pl.BlockSpec((None, tn, k), lambda b, i, j: (l, j, 0)),
        ],
        out_specs=pl.BlockSpec((None, tm, tn), lambda b, i, j: (b, i, j)),
        out_shape=jax.ShapeDtypeStruct((bv, lv, ncols), out_dtype),
        compiler_params=_cparams(("parallel", "parallel", "arbitrary")),
        name=name,
    )(a, w_t)


def _in_proj_att_kernel(a_ref, w_ref, y_ref, k_ref, v_ref):
    j = pl.program_id(2)

    @pl.when(j == 1)
    def _():
        k_ref[...] = _dot_nt(a_ref[...], w_ref[...])

    @pl.when(j == 2)
    def _():
        v_ref[...] = _dot_nt(a_ref[...], w_ref[...])

    @pl.when(j == 0)
    def _():
        y_ref[...] = _dot_nt(a_ref[...], w_ref[...])


def _in_proj_att(path, a, w_t, l, tn, tm=None):
    bv, lv, k = a.shape
    tm = path.tm if tm is None else tm
    out_spec = pl.BlockSpec((None, tm, tn), lambda b, i, j: (b, i, 0))
    out_shape = jax.ShapeDtypeStruct((bv, lv, tn), F32)
    return pl.pallas_call(
        _in_proj_att_kernel,
        grid=(bv, lv // tm, 3),
        in_specs=[
            pl.BlockSpec((None, tm, k), lambda b, i, j: (b, i, 0)),
            pl.BlockSpec((None, tn, k), lambda b, i, j: (l, j, 0)),
        ],
        out_specs=[out_spec, out_spec, out_spec],
        out_shape=[out_shape, out_shape, out_shape],
        compiler_params=_cparams(("parallel", "parallel", "arbitrary")),
        name="in_proj_att",
    )(a, w_t)


def _mm_resid_kernel(a_ref, w_ref, x_ref, gt_ref, o_ref):
    o_ref[...] = x_ref[...] + gt_ref[...] * _dot(a_ref[...], w_ref[...])


def _mm_resid(path, a, w, x, mod_all, l, k_gt, tn, tm=None, name="mm_resid"):
    bv, lv, k = a.shape
    d = w.shape[-1]
    tm = path.tm if tm is None else tm
    p = _Path(bv, lv, tm, path.per_row_mod, path.mod_row0)
    return pl.pallas_call(
        _mm_resid_kernel,
        grid=(bv, lv // tm, d // tn),
        in_specs=[
            pl.BlockSpec((None, tm, k), lambda b, i, j: (b, i, 0)),
            pl.BlockSpec((None, k, tn), lambda b, i, j: (l, 0, j)),
            pl.BlockSpec((None, tm, tn), lambda b, i, j: (b, i, j)),
            p.mod_spec(l, k_gt, d, tn=tn, with_j=True),
        ],
        out_specs=pl.BlockSpec((None, tm, tn), lambda b, i, j: (b, i, j)),
        out_shape=jax.ShapeDtypeStruct((bv, lv, d), F32),
        compiler_params=_cparams(("parallel", "parallel", "arbitrary")),
        name=name,
    )(a, w, x, p.mod_operand(mod_all))


def _swiglu_kernel(a_ref, wg_ref, wu_ref, o_ref):
    a = a_ref[...]
    o_ref[...] = (_silu(_dot(a, wg_ref[...])) * _dot(a, wu_ref[...])).astype(o_ref.dtype)


def _swiglu(path, a, w_gu, l, tn, tm=None):
    bv, lv, k = a.shape
    d_ff = w_gu.shape[-1] // 2
    tm = path.tm if tm is None else tm
    nj = d_ff // tn
    assert d_ff % tn == 0
    return pl.pallas_call(
        _swiglu_kernel,
        grid=(bv, lv // tm, nj),
        in_specs=[
            pl.BlockSpec((None, tm, k), lambda b, i, j: (b, i, 0)),
            pl.BlockSpec((None, k, tn), lambda b, i, j: (l, 0, j)),
            pl.BlockSpec((None, k, tn), lambda b, i, j: (l, 0, nj + j)),
        ],
        out_specs=pl.BlockSpec((None, tm, tn), lambda b, i, j: (b, i, j)),
        out_shape=jax.ShapeDtypeStruct((bv, lv, d_ff), BF16),
        compiler_params=_cparams(("parallel", "parallel", "arbitrary")),
        name="swiglu",
    )(a, w_gu, w_gu)


def _merge_kernel(h_ref, oa_ref, ob_ref, oc_ref, ga_ref, gb_ref, gc_ref, wa_ref, wb_ref, wc_ref, o_ref):
    h = h_ref[...]
    m = jax.nn.sigmoid(_dot_nt(h, ga_ref[...])) * _dot(oa_ref[...], wa_ref[...])
    m = m + jax.nn.sigmoid(_dot_nt(h, gb_ref[...])) * _dot(ob_ref[...], wb_ref[...])
    m = m + jax.nn.sigmoid(_dot_nt(h, gc_ref[...])) * _dot(oc_ref[...], wc_ref[...])
    o_ref[...] = m.astype(o_ref.dtype)


def _merge(path, h, o_a, o_b, o_c, w_t, gate_row0, w_br, l, tn, tm=None):
    bv, lv, wa = o_a.shape
    wb, wc = o_b.shape[-1], o_c.shape[-1]
    k = h.shape[-1]
    d = w_br.shape[-1]
    tm = path.tm if tm is None else tm
    g0 = gate_row0 // tn
    per = d // tn
    assert gate_row0 % tn == 0 and wb == wa and wc == 2 * wa
    return pl.pallas_call(
        _merge_kernel,
        grid=(bv, lv // tm, d // tn),
        in_specs=[
            pl.BlockSpec((None, tm, k), lambda b, i, j: (b, i, 0)),
            pl.BlockSpec((None, tm, wa), lambda b, i, j: (b, i, 0)),
            pl.BlockSpec((None, tm, wb), lambda b, i, j: (b, i, 0)),
            pl.BlockSpec((None, tm, wc), lambda b, i, j: (b, i, 0)),
            pl.BlockSpec((None, tn, k), lambda b, i, j: (l, g0 + j, 0)),
            pl.BlockSpec((None, tn, k), lambda b, i, j: (l, g0 + per + j, 0)),
            pl.BlockSpec((None, tn, k), lambda b, i, j: (l, g0 + 2 * per + j, 0)),
            pl.BlockSpec((None, wa, tn), lambda b, i, j: (l, 0, j)),
            pl.BlockSpec((None, wb, tn), lambda b, i, j: (l, 1, j)),
            pl.BlockSpec((None, wc, tn), lambda b, i, j: (l, 1, j)),
        ],
        out_specs=pl.BlockSpec((None, tm, tn), lambda b, i, j: (b, i, j)),
        out_shape=jax.ShapeDtypeStruct((bv, lv, d), BF16),
        compiler_params=_cparams(("parallel", "parallel", "arbitrary")),
        name="merge",
    )(h, o_a, o_b, o_c, w_t, w_t, w_t, w_br, w_br, w_br)


def _lb_from_gamma(gam, l):
    depth = len(gam)
    m = gam[0]
    for i in range(1, depth):
        m = jnp.maximum(m, gam[i])
    e = [jnp.exp(gam[i] - m) for i in range(depth)]
    tot = e[0]
    for i in range(1, depth):
        tot = tot + e[i]
    sm = [ei / tot for ei in e]
    cs = [sm[0]]
    for i in range(1, depth):
        cs.append(cs[-1] + sm[i])
    return cs[l] - cs[0]


def _rec_core(qq, kk, logf, v_ref_tile, g_tile, gn, st_ref, o_ref, *, chunk, n_groups, hpg):
    c = chunk
    wq = n_groups * LANES
    dk = LANES // hpg
    n_levels = int(math.log2(c))
    row = lax.broadcasted_iota(jnp.int32, (c, wq), 0)
    ti = lax.broadcasted_iota(jnp.int32, (c, c), 0)
    si = lax.broadcasted_iota(jnp.int32, (c, c), 1)
    x = jnp.bitwise_xor(ti, si)
    lvl = jnp.zeros((c, c), jnp.int32)
    for lev in range(1, n_levels + 1):
        lvl = lvl + jnp.where(x >= (1 << (lev - 1)), 1, 0)
    lvl = jnp.where(ti > si, lvl, -1)
    lane = lax.broadcasted_iota(jnp.int32, (c, LANES), 1)

    def head_mask(a, sub):
        if hpg == 1:
            return a
        return jnp.where(lane < dk, a, 0.0) if sub == 0 else jnp.where(lane >= dk, a, 0.0)

    pre = logf
    tot = logf
    ql, kl = [], []
    for lev in range(1, n_levels + 1):
        half = 1 << (lev - 1)
        upper = (row & half) != 0
        e = jnp.exp(jnp.where(upper, pre, tot - pre))
        ql.append(qq * e)
        kl.append(kk * e)
        up = pltpu.roll(tot, half, 0)
        dn = pltpu.roll(tot, c - half, 0)
        pre = pre + jnp.where(upper, up, 0.0)
        tot = tot + jnp.where(upper, up, dn)
    q_in = qq * jnp.exp(pre)
    k_out = kk * jnp.exp(tot - pre)
    d_all = jnp.exp(tot[0:1, :])
    qk = qq * kk

    for g in range(n_groups):
        gs = slice(g * LANES, (g + 1) * LANES)
        st = st_ref[g]
        st_bf = st.astype(BF16)
        st_new = st * d_all[:, gs]
        for sub in range(hpg):
            h = g * hpg + sub
            hs = slice(h * LANES, (h + 1) * LANES)
            v_bf = v_ref_tile[:, hs].astype(BF16)
            att = jnp.zeros((c, c), F32)
            for lev in range(1, n_levels + 1):
                p = _dot_nt(head_mask(ql[lev - 1][:, gs], sub).astype(BF16), kl[lev - 1][:, gs].astype(BF16))
                att = jnp.where(lvl == lev, p, att)
            diag = jnp.sum(head_mask(qk[:, gs], sub), axis=-1, keepdims=True)
            att = jnp.where(x == 0, diag, att)
            o = _dot_nt(head_mask(q_in[:, gs], sub).astype(BF16), st_bf) + _dot(att.astype(BF16), v_bf)
            st_new = st_new + _dot_tn(v_bf, head_mask(k_out[:, gs], sub).astype(BF16))
            y = o * lax.rsqrt(jnp.mean(o * o, axis=-1, keepdims=True) + EPS) * gn
            o_ref[:, hs] = (y * _silu(g_tile[:, hs])).astype(o_ref.dtype)
        st_ref[g] = st_new


def _state_in(s0_ref, st_ref, n_groups, hpg):
    dk = LANES // hpg
    for g in range(n_groups):
        blk = jnp.concatenate([s0_ref[g * hpg + sub] for sub in range(hpg)], axis=0) if hpg > 1 else s0_ref[g]
        st_ref[g] = blk.T


def _state_out(st_ref, sout_ref, n_groups, hpg):
    dk = LANES // hpg
    for g in range(n_groups):
        t = st_ref[g].T
        for sub in range(hpg):
            sout_ref[g * hpg + sub] = t[sub * dk:(sub + 1) * dk, :]


def _rec_a_kernel(q_ref, f_ref, i_ref, g_ref, lbg_ref, gn_ref, s0_ref, o_ref, sout_ref, st_ref, *, l, chunk):
    ci = pl.program_id(1)

    @pl.when(ci == 0)
    def _():
        _state_in(s0_ref, st_ref, 4, 1)

    lb = _lb_from_gamma([lbg_ref[i:i + 1, :] for i in range(lbg_ref.shape[0])], l)
    f = f_ref[...]
    logf = _logaddexp(jnp.log(lb), jnp.log1p(-lb) + _log_sigmoid(f))
    kk = (1.0 - lb) * jax.nn.sigmoid(-f)
    qq = _silu(q_ref[...]) * (LANES ** -0.5)
    _rec_core(qq, kk, logf, i_ref, g_ref[...], gn_ref[...], st_ref, o_ref, chunk=chunk, n_groups=4, hpg=1)

    @pl.when(ci == pl.num_programs(1) - 1)
    def _():
        _state_out(st_ref, sout_ref, 4, 1)


def _rec_b_kernel(q_ref, k_ref, v_ref, g_ref, r_ref, wgk_ref, bgk_ref, gn_ref, s0_ref, o_ref, sout_ref, st_ref,
                  *, chunk):
    ci = pl.program_id(1)

    @pl.when(ci == 0)
    def _():
        _state_in(s0_ref, st_ref, 2, 2)

    gk = _dot(r_ref[...].astype(BF16), wgk_ref[...]) + bgk_ref[...]
    logf = _log_sigmoid(gk) * (1.0 / GK_NORM)
    qq = q_ref[...] * ((LANES // 2) ** -0.5)
    _rec_core(qq, k_ref[...], logf, v_ref, g_ref[...], gn_ref[...], st_ref, o_ref, chunk=chunk, n_groups=2, hpg=2)

    @pl.when(ci == pl.num_programs(1) - 1)
    def _():
        _state_out(st_ref, sout_ref, 2, 2)


def _rec_a(y, lb_gamma, gn_a, s0, l, chunk):
    b, L, _ = y.shape
    depth = lb_gamma.shape[0]
    w = 4 * LANES
    col = lambda k: pl.BlockSpec((None, chunk, w), lambda bi, ci: (bi, ci, k))
    return pl.pallas_call(
        functools.partial(_rec_a_kernel, l=l, chunk=chunk),
        grid=(b, L // chunk),
        in_specs=[
            col(0), col(1), col(2), col(3),
            pl.BlockSpec((depth, w), lambda bi, ci: (0, 0)),
            pl.BlockSpec((None, 1, LANES), lambda bi, ci: (l, 0, 0)),
            pl.BlockSpec((None, 4, LANES, LANES), lambda bi, ci: (bi, 0, 0, 0)),
        ],
        out_specs=[
            pl.BlockSpec((None, chunk, w), lambda bi, ci: (bi, ci, 0)),
            pl.BlockSpec((None, 4, LANES, LANES), lambda bi, ci: (bi, 0, 0, 0)),
        ],
        out_shape=[jax.ShapeDtypeStruct((b, L, w), BF16), jax.ShapeDtypeStruct(s0.shape, F32)],
        scratch_shapes=[pltpu.VMEM((4, LANES, LANES), F32)],
        compiler_params=_cparams(("parallel", "arbitrary")),
        name="rec_hgrn",
    )(y, y, y, y, lb_gamma, gn_a.reshape(depth, 1, LANES), s0)


def _rec_b(y, r, wgk_p, b_gk, gn_b, s0, l, chunk, col_b):
    b, L, _ = y.shape
    depth = gn_b.shape[0]
    c256 = col_b // 256
    c512 = (col_b + 512) // 512
    return pl.pallas_call(
        functools.partial(_rec_b_kernel, chunk=chunk),
        grid=(b, L // chunk),
        in_specs=[
            pl.BlockSpec((None, chunk, 256), lambda bi, ci: (bi, ci, c256)),
            pl.BlockSpec((None, chunk, 256), lambda bi, ci: (bi, ci, c256 + 1)),
            pl.BlockSpec((None, chunk, 512), lambda bi, ci: (bi, ci, c512)),
            pl.BlockSpec((None, chunk, 512), lambda bi, ci: (bi, ci, c512 + 1)),
            pl.BlockSpec((None, chunk, LANES), lambda bi, ci: (bi, ci, 0)),
            pl.BlockSpec((None, LANES, 256), lambda bi, ci: (l, 0, 0)),
            pl.BlockSpec((None, 1, 256), lambda bi, ci: (l, 0, 0)),
            pl.BlockSpec((None, 1, LANES), lambda bi, ci: (l, 0, 0)),
            pl.BlockSpec((None, 4, 64, LANES), lambda bi, ci: (bi, 0, 0, 0)),
        ],
        out_specs=[
            pl.BlockSpec((None, chunk, 512), lambda bi, ci: (bi, ci, 0)),
            pl.BlockSpec((None, 4, 64, LANES), lambda bi, ci: (bi, 0, 0, 0)),
        ],
        out_shape=[jax.ShapeDtypeStruct((b, L, 512), BF16), jax.ShapeDtypeStruct(s0.shape, F32)],
        scratch_shapes=[pltpu.VMEM((2, LANES, LANES), F32)],
        compiler_params=_cparams(("parallel", "arbitrary")),
        name="rec_gla",
    )(y, y, y, y, r, wgk_p, b_gk.reshape(depth, 1, 256), gn_b.reshape(depth, 1, LANES), s0)


def _dec_finish(s_new, q_col, g_row, gn, o_ref, h):
    o = jnp.sum(q_col * s_new, axis=0, keepdims=True)
    y = o * lax.rsqrt(jnp.mean(o * o, axis=-1, keepdims=True) + EPS) * gn
    o_ref[h] = y * _silu(g_row)


def _dec_a_kernel(q_ref, f_ref, i_ref, g_ref, lbg_ref, gn_ref, s0_ref, o_ref, sout_ref, *, l):
    lb_all = _lb_from_gamma([lbg_ref[i] for i in range(lbg_ref.shape[0])], l)
    for h in range(4):
        lb = lb_all[h]
        f = f_ref[h]
        logf = _logaddexp(jnp.log(lb), jnp.log1p(-lb) + _log_sigmoid(f))
        kk = (1.0 - lb) * jax.nn.sigmoid(-f)
        qq = _silu(q_ref[h]) * (LANES ** -0.5)
        s_new = jnp.exp(logf) * s0_ref[h] + kk * i_ref[h]
        sout_ref[h] = s_new
        _dec_finish(s_new, qq, g_ref[h], gn_ref[...], o_ref, h)


def _dec_b_kernel(q_ref, k_ref, v_ref, g_ref, r_ref, wgk_ref, bgk_ref, gn_ref, s0_ref, o_ref, sout_ref):
    r = r_ref[...]
    for h in range(4):
        gk = jnp.sum(wgk_ref[h] * r, axis=-1, keepdims=True) + bgk_ref[h]
        logf = _log_sigmoid(gk) * (1.0 / GK_NORM)
        qq = q_ref[h] * ((LANES // 2) ** -0.5)
        s_new = jnp.exp(logf) * s0_ref[h] + k_ref[h] * v_ref[h]
        sout_ref[h] = s_new
        _dec_finish(s_new, qq, g_ref[h], gn_ref[...], o_ref, h)


def _dec_a(qa, fa, ia, ga, lb_gamma, gn_a, s0, l):
    nb = qa.shape[0]
    depth = lb_gamma.shape[0]
    colspec = pl.BlockSpec((None, 4, LANES, 1), lambda b: (b, 0, 0, 0))
    rowspec = pl.BlockSpec((None, 4, 1, LANES), lambda b: (b, 0, 0, 0))
    stspec = pl.BlockSpec((None, 4, LANES, LANES), lambda b: (b, 0, 0, 0))
    o, s = pl.pallas_call(
        functools.partial(_dec_a_kernel, l=l),
        grid=(nb,),
        in_specs=[colspec, colspec, rowspec, rowspec,
                  pl.BlockSpec((depth, 4, LANES, 1), lambda b: (0, 0, 0, 0)),
                  pl.BlockSpec((None, 1, LANES), lambda b: (l, 0, 0)),
                  stspec],
        out_specs=[rowspec, stspec],
        out_shape=[jax.ShapeDtypeStruct((nb, 4, 1, LANES), F32), jax.ShapeDtypeStruct(s0.shape, F32)],
        compiler_params=_cparams(("parallel",)),
        name="dec_hgrn",
    )(qa.reshape(nb, 4, LANES, 1), fa.reshape(nb, 4, LANES, 1), ia.reshape(nb, 4, 1, LANES),
      ga.reshape(nb, 4, 1, LANES), lb_gamma.reshape(depth, 4, LANES, 1), gn_a.reshape(depth, 1, LANES), s0)
    return o.reshape(nb, 1, 4 * LANES).astype(BF16), s


def _dec_b(qb, kb, vb, gb, rb, wgk_t, b_gk, gn_b, s0, l):
    nb = qb.shape[0]
    depth = gn_b.shape[0]
    rank = rb.shape[-1]
    colspec = pl.BlockSpec((None, 4, 64, 1), lambda b: (b, 0, 0, 0))
    rowspec = pl.BlockSpec((None, 4, 1, LANES), lambda b: (b, 0, 0, 0))
    stspec = pl.BlockSpec((None, 4, 64, LANES), lambda b: (b, 0, 0, 0))
    o, s = pl.pallas_call(
        _dec_b_kernel,
        grid=(nb,),
        in_specs=[colspec, colspec, rowspec, rowspec,
                  pl.BlockSpec((None, 1, rank), lambda b: (b, 0, 0)),
                  pl.BlockSpec((None, 4, 64, rank), lambda b: (l, 0, 0, 0)),
                  pl.BlockSpec((None, 4, 64, 1), lambda b: (l, 0, 0, 0)),
                  pl.BlockSpec((None, 1, LANES), lambda b: (l, 0, 0)),
                  stspec],
        out_specs=[rowspec, stspec],
        out_shape=[jax.ShapeDtypeStruct((nb, 4, 1, LANES), F32), jax.ShapeDtypeStruct(s0.shape, F32)],
        compiler_params=_cparams(("parallel",)),
        name="dec_gla",
    )(qb.reshape(nb, 4, 64, 1), kb.reshape(nb, 4, 64, 1), vb.reshape(nb, 4, 1, LANES),
      gb.reshape(nb, 4, 1, LANES), rb.reshape(nb, 1, rank), wgk_t, b_gk.reshape(depth, 4, 64, 1),
      gn_b.reshape(depth, 1, LANES), s0)
    return o.reshape(nb, 1, 4 * LANES).astype(BF16), s


def _suffix_matrix(n, rows_first):
    a = lax.broadcasted_iota(jnp.int32, (n, n), 0)
    b = lax.broadcasted_iota(jnp.int32, (n, n), 1)
    if rows_first:
        m = jnp.where(a >= b, 1.0, 0.0).astype(BF16)
        return jnp.concatenate([m, m], axis=0)
    m = jnp.where(b >= a, 1.0, 0.0).astype(BF16)
    return jnp.concatenate([m, m], axis=1)


def _sb_prompt_kernel(bias_ref, qa_ref, qb_ref, k_ref, v_ref, oa_ref, ob_ref,
                      kbf_ref, vbf_ref, uu_ref, z_ref, sp_ref, suf_ref, *, l, tq, nq, scale, unroll):
    h = pl.program_id(1)
    i = pl.program_id(2)
    n_off = nq - 1

    @pl.when(i == 0)
    def _():
        kbf_ref[...] = k_ref[...].astype(BF16)
        vbf_ref[...] = v_ref[...].astype(BF16)
        uu_ref[...] = _suffix_matrix(tq, rows_first=True)[0:tq]

    bias = bias_ref[l, h]
    qa = qa_ref[...].astype(BF16)
    qb = qb_ref[...].astype(BF16)
    uu = uu_ref[...]
    ti = lax.broadcasted_iota(jnp.int32, (tq, tq), 0)
    si = lax.broadcasted_iota(jnp.int32, (tq, tq), 1)
    causal = si < ti

    def off_kb(s):
        return jnp.where(s < i, i - 1 - s, nq - 2 - s)

    def logits(slot, q, kb, masked):
        start = pl.multiple_of(kb * tq, tq)
        z = _dot_nt(q, kbf_ref[pl.ds(start, tq), :]) * scale + bias
        neg_abs = lax.bitcast_convert_type(
            lax.bitcast_convert_type(z, jnp.uint32) | jnp.uint32(0x80000000), F32)
        sp = jnp.maximum(z, 0.0) + jnp.log(1.0 + jnp.exp(neg_abs))
        if masked:
            sp = jnp.where(causal, sp, 0.0)
        z_ref[slot] = z
        sp_ref[slot] = sp.astype(BF16)

    def suffix(slot):
        suf_ref[slot] = _dot(sp_ref[slot], uu)

    def weigh(slot, kb, run, acc, diagonal):
        start = pl.multiple_of(kb * tq, tq)
        suf = suf_ref[slot]
        if diagonal:
            w = jnp.where(causal, jnp.exp(z_ref[slot] - suf), 0.0)
            return suf[:, 0:1], _dot(w.astype(BF16), vbf_ref[pl.ds(start, tq), :])
        w = jnp.exp(z_ref[slot] - suf - run)
        return run + suf[:, 0:1], acc + _dot(w.astype(BF16), vbf_ref[pl.ds(start, tq), :])

    def sweep(diagonals, fn):
        diagonals()
        for u in range(unroll):
            fn(u)

        def body(j, c):
            for u in range(unroll):
                fn(unroll * j + u)
            return c
        lax.fori_loop(1, n_off // unroll, body, 0)

    sweep(lambda: (logits(0, qa, i, True), logits(1, qb, nq - 1 - i, True)),
          lambda s: logits(2 + s, jnp.where(s < i, qa, qb), off_kb(s), False))
    sweep(lambda: (suffix(0), suffix(1)), lambda s: suffix(2 + s))

    run_a, acc_a = weigh(0, i, None, None, True)
    run_b, acc_b = weigh(1, nq - 1 - i, None, None, True)

    def chain(s, c):
        run, acc, out_a = c
        at_b = s == i
        out_a = jnp.where(at_b, acc, out_a)
        run = jnp.where(at_b, run_b, run)
        acc = jnp.where(at_b, acc_b, acc)
        run, acc = weigh(2 + s, off_kb(s), run, acc, False)
        return run, acc, out_a

    def body3(j, c):
        for u in range(unroll):
            c = chain(unroll * j + u, c)
        return c

    c = (run_a, acc_a, acc_a)
    for u in range(unroll):
        c = chain(u, c)
    _, acc, out_a = lax.fori_loop(1, n_off // unroll, body3, c)
    oa_ref[...] = out_a.astype(oa_ref.dtype)
    ob_ref[...] = acc.astype(ob_ref.dtype)


def _sb_prompt(y, kbuf, vbuf, b_sb, l, col_q, n_heads, tq):
    b, L, _ = y.shape
    cq = col_q // LANES
    nq = L // tq
    n_off = nq - 1
    assert nq % 2 == 0
    unroll = max(u for u in (1, 2, 3, 4, 5, 15) if n_off % u == 0)
    half = nq // 2
    n_slots = nq + 1
    o_lo, o_hi = pl.pallas_call(
        functools.partial(_sb_prompt_kernel, l=l, tq=tq, nq=nq, scale=LANES ** -0.5, unroll=unroll),
        grid=(b, n_heads, half),
        in_specs=[
            pl.BlockSpec(memory_space=pltpu.SMEM),
            pl.BlockSpec((None, tq, LANES), lambda bi, h, i: (bi, i, cq + h)),
            pl.BlockSpec((None, tq, LANES), lambda bi, h, i: (bi, nq - 1 - i, cq + h)),
            pl.BlockSpec((None, L, LANES), lambda bi, h, i: (bi, 0, h)),
            pl.BlockSpec((None, L, LANES), lambda bi, h, i: (bi, 0, h)),
        ],
        out_specs=[pl.BlockSpec((None, tq, LANES), lambda bi, h, i: (bi, i, h)),
                   pl.BlockSpec((None, tq, LANES), lambda bi, h, i: (bi, half - 1 - i, h))],
        out_shape=[jax.ShapeDtypeStruct((b, L // 2, n_heads * LANES), BF16)] * 2,
        scratch_shapes=[pltpu.VMEM((L, LANES), BF16), pltpu.VMEM((L, LANES), BF16),
                        pltpu.VMEM((tq, tq), BF16),
                        pltpu.VMEM((n_slots, tq, tq), F32), pltpu.VMEM((n_slots, tq, tq), BF16),
                        pltpu.VMEM((n_slots, tq, tq), F32)],
        compiler_params=_cparams(("parallel", "parallel", "arbitrary")),
        name="sb_prompt",
    )(b_sb, y, y, kbuf, vbuf)
    return jnp.concatenate([o_lo, o_hi], axis=1)


def _sb_decode_kernel(pt_ref, bias_ref, q_ref, *refs, page, npg, scale):
    k_refs, v_refs = refs[:npg], refs[npg:2 * npg]
    o_ref, run_ref, acc_ref = refs[2 * npg:]
    p = pl.program_id(1)

    @pl.when(p == 0)
    def _():
        run_ref[...] = jnp.zeros_like(run_ref)
        acc_ref[...] = jnp.zeros_like(acc_ref)

    q = q_ref[...]
    bias = bias_ref[...]
    nh = q.shape[0]
    tok = lax.broadcasted_iota(jnp.int32, (page, nh, LANES), 0)
    lane = lax.broadcasted_iota(jnp.int32, (page, nh, LANES), 2)
    own_lane = tok == lane
    ones = jnp.ones((LANES, LANES), BF16)
    uu = _suffix_matrix(page, rows_first=True)
    run = run_ref[...]
    acc = acc_ref[...]
    for i in range(npg):
        prod = (k_refs[i][...] * q[None]).reshape(page * nh, LANES).astype(BF16)
        zrep = _dot(prod, ones).reshape(page, nh, LANES)
        z = jnp.sum(jnp.where(own_lane, zrep, 0.0), axis=0) * scale + bias
        lk = -(jnp.maximum(z, 0.0) + jnp.log(1.0 + jnp.exp(-jnp.abs(z))))
        hi = lk.astype(BF16)
        lo = (lk - hi.astype(F32)).astype(BF16)
        suf = _dot(jnp.concatenate([hi, lo], axis=1), uu)
        w = jnp.exp(z + suf + run)
        wsel = jnp.where(own_lane, w[None], 0.0).reshape(page * nh, LANES).astype(BF16)
        wrep = _dot(wsel, ones).reshape(page, nh, LANES)
        acc = acc + jnp.sum(wrep * v_refs[i][...], axis=0)
        run = run + suf[:, 0:1]
    run_ref[...] = run
    acc_ref[...] = acc

    @pl.when(p == pl.num_programs(1) - 1)
    def _():
        o_ref[...] = acc


def _sb_decode(q, cache_k, cache_v, page_table, b_sb, l):
    nb = q.shape[0]
    _, _, page, nh, dh = cache_k.shape
    n_pages = page_table.shape[1]
    npg = math.gcd(DEC_PAGES, n_pages)
    assert nh == SUBLANES and dh == LANES and page == LANES
    bias = jnp.broadcast_to(b_sb[l][:, None], (nh, LANES))

    def page_spec(i):
        return pl.BlockSpec((None, None, page, nh, dh),
                            lambda b, p, pt: (l, pt[b, n_pages - 1 - (p * npg + i)], 0, 0, 0))

    grid_spec = pltpu.PrefetchScalarGridSpec(
        num_scalar_prefetch=1,
        grid=(nb, n_pages // npg),
        in_specs=[pl.BlockSpec((nh, LANES), lambda b, p, pt: (0, 0)),
                  pl.BlockSpec((None, nh, dh), lambda b, p, pt: (b, 0, 0))]
                 + [page_spec(i) for i in range(npg)] * 2,
        out_specs=pl.BlockSpec((None, nh, dh), lambda b, p, pt: (b, 0, 0)),
        scratch_shapes=[pltpu.VMEM((nh, LANES), F32), pltpu.VMEM((nh, dh), F32)],
    )
    o = pl.pallas_call(
        functools.partial(_sb_decode_kernel, page=page, npg=npg, scale=dh ** -0.5),
        grid_spec=grid_spec,
        out_shape=jax.ShapeDtypeStruct((nb, nh, dh), F32),
        compiler_params=_cparams(("parallel", "arbitrary")),
        name="sb_decode",
    )(page_table, bias, q.reshape(nb, nh, dh), *([cache_k] * npg), *([cache_v] * npg))
    return o.reshape(nb, 1, nh * dh).astype(BF16)


def _trunk(path, x, mod_all, wts, state_a, state_b, cache, dims):
    d_model = x.shape[-1]
    depth = wts["w_o"].shape[0]
    c1, c2 = dims["col_rec"], dims["col_att"]
    is_prompt = cache is None
    tm = path.tm
    ks, vs, sas, sbs = [], [], [], []
    for l in range(depth):
        h = _norm_mod(path, x, wts["g_mix"], mod_all, l, 1, 0)
        y1 = _mm_nt(path, h, wts["w_in_rec"], l, c1["end"] // 2, name="in_proj_rec")
        y2, kbuf, vbuf = _in_proj_att(path, h, wts["w_in_att"], l, c2["kc"])
        r = _mm_nt(path, h, wts["w_in_rb"], l, LANES, name="in_proj_r")
        bv, lv, _ = y1.shape
        if is_prompt:
            o_a, s_a = _rec_a(y1, wts["lb_gamma"], wts["gn_a"], state_a[l], l, dims["chunk"])
            o_b, s_b = _rec_b(y1, r, wts["w_gk_p"], wts["b_gk"], wts["gn_b"], state_b[l], l, dims["chunk"], c1["qb"])
            o_c = _sb_prompt(y2, kbuf, vbuf, wts["b_sb"], l, c2["qc"], dims["h_c"], dims["tq"])
        else:
            f1, f2 = y1.reshape(lv, -1), y2.reshape(lv, -1)
            s1 = lambda a, b_: f1[:, c1[a]:c1[b_]]
            o_a, s_a = _dec_a(s1("qa", "fa"), s1("fa", "ia"), s1("ia", "ga"), s1("ga", "qb"),
                              wts["lb_gamma"], wts["gn_a"], state_a[l], l)
            o_b, s_b = _dec_b(s1("qb", "kb"), s1("kb", "vb"), s1("vb", "gb"), s1("gb", "end"),
                              r.reshape(lv, -1)[:, :dims["gk_rank"]], wts["w_gk_t"], wts["b_gk"], wts["gn_b"],
                              state_b[l], l)
            o_c = _sb_decode(f2, cache[0], cache[1], cache[2], wts["b_sb"], l)
            o_a, o_b, o_c = (o.reshape(bv, lv, -1) for o in (o_a, o_b, o_c))
        merged = _merge(path, h, o_a, o_b, o_c, wts["w_in_att"], c2["ma"], wts["w_br"], l, 512)
        x = _mm_resid(path, merged, wts["w_o"], x, mod_all, l, 2, 1024, name="out_proj")
        h2 = _norm_mod(path, x, wts["g_ffn"], mod_all, l, 4, 3)
        act = _swiglu(path, h2, wts["w_gu"], l, 512)
        x = _mm_resid(path, act, wts["w_down"], x, mod_all, l, 5, 512, name="down_proj")
        ks.append(kbuf)
        vs.append(vbuf)
        sas.append(s_a)
        sbs.append(s_b)
    xf = _final_norm(x, wts["g_final"], tm)
    return xf, jnp.stack(ks), jnp.stack(vs), jnp.stack(sas), jnp.stack(sbs)


def kernel(x_prompt, x_sample, cache_k, cache_v, state_hgrn, state_gla, page_table, c_prompt, c_sample, w_ada, b_ada, g_mix, w_in, b_sb, lb_gamma, gn_a, w_gk, b_gk, gn_b, w_br, w_o, g_ffn, w_gu, w_down, g_final):
    bp, seq, d_model = x_prompt.shape
    nb = x_sample.shape[0]
    depth, _, _, h_c, dh_c = cache_k.shape[0], None, None, cache_k.shape[3], cache_k.shape[4]
    h_a, dk_a, dv_a = state_hgrn.shape[2:]
    h_b, dk_b, dv_b = state_gla.shape[2:]
    gk_rank = w_gk.shape[1]
    w_a, w_b, w_c = h_a * dv_a, h_b * dv_b, h_c * dh_c

    def offsets(widths):
        col, off = {}, 0
        for name, wd in widths:
            col[name] = off
            off += wd
        col["end"] = off
        return col

    col_rec = offsets([("qa", h_a * dk_a), ("fa", h_a * dk_a), ("ia", w_a), ("ga", w_a),
                       ("qb", h_b * dk_b), ("kb", h_b * dk_b), ("vb", w_b), ("gb", w_b)])
    col_att = offsets([("qc", w_c), ("kc", w_c), ("vc", w_c), ("ma", d_model), ("mb", d_model), ("mc", d_model)])
    src_rb = col_rec["end"]
    w_in_t = jnp.swapaxes(w_in, 1, 2)
    w_in_rb = jnp.concatenate([w_in_t[:, src_rb:src_rb + gk_rank],
                               jnp.zeros((depth, LANES - gk_rank, d_model), w_in.dtype)], axis=1).astype(BF16)
    w_gk_p = jnp.concatenate([w_gk, jnp.zeros((depth, LANES - gk_rank, w_gk.shape[-1]), w_gk.dtype)],
                             axis=1).astype(BF16)
    wts = {
        "w_in_rec": w_in_t[:, :src_rb].astype(BF16), "w_in_att": w_in_t[:, src_rb + gk_rank:].astype(BF16),
        "w_in_rb": w_in_rb,
        "w_br": w_br.astype(BF16), "w_o": w_o.astype(BF16), "w_gu": w_gu.astype(BF16),
        "w_down": w_down.astype(BF16), "w_gk_p": w_gk_p,
        "w_gk_t": jnp.swapaxes(w_gk, 1, 2).reshape(depth, h_b, dk_b, gk_rank),
        "g_mix": g_mix, "g_ffn": g_ffn, "g_final": g_final, "b_sb": b_sb, "lb_gamma": lb_gamma,
        "gn_a": gn_a, "gn_b": gn_b, "b_gk": b_gk,
    }
    dims = {"col_rec": col_rec, "col_att": col_att, "chunk": 256, "tq": 256, "h_c": h_c, "gk_rank": gk_rank}

    c_all = jnp.concatenate([c_sample, c_prompt, jnp.zeros((16 - nb - bp, d_model), F32)], axis=0)
    mod_all = _ada_mod(c_all, w_ada, b_ada)

    prompt = _Path(bp, seq, min(1024, seq), per_row_mod=False, mod_row0=nb)
    sample = _Path(1, nb, nb, per_row_mod=True, mod_row0=0)

    zeros_a = jnp.zeros((depth, bp) + state_hgrn.shape[2:], state_hgrn.dtype)
    zeros_b = jnp.zeros((depth, bp) + state_gla.shape[2:], state_gla.dtype)
    y_p, k_p, v_p, sa_p, sb_p = _trunk(prompt, x_prompt, mod_all, wts, zeros_a, zeros_b, None, dims)

    cache = (cache_k, cache_v, page_table)
    y_s, k_s, v_s, sa_s, sb_s = _trunk(sample, x_sample.reshape(1, nb, d_model), mod_all, wts,
                                       state_hgrn, state_gla, cache, dims)

    return (y_p, y_s.reshape(nb, 1, d_model),
            k_p.reshape(depth, bp, seq, h_c, dh_c), v_p.reshape(depth, bp, seq, h_c, dh_c),
            k_s.reshape(depth, nb, 1, h_c, dh_c), v_s.reshape(depth, nb, 1, h_c, dh_c),
            sa_p, sa_s, sb_p, sb_s)
```
